```python
import math
import jax, jax.numpy as jnp
from jax import lax
import numpy as np

D_MODEL = 2048
BATCH = 1
SEQ = 8192
DEPTH = 4

CHUNK = 64
Q_BLOCK = 128
ROPE_THETA = 500000.0
NORM_EPS = 1e-6

LRU_WIDTH = 768
LRU_BLOCKS = 6
LRU_BLOCK_W = LRU_WIDTH // LRU_BLOCKS
CONV_W = 4
LRU_C = 8.0

DIFF_HEADS = 4
DIFF_HEAD_DIM = 64
DIFF_V_DIM = 2 * DIFF_HEAD_DIM
DIFF_QK = DIFF_HEADS * 2 * DIFF_HEAD_DIM
DIFF_OUT = DIFF_HEADS * DIFF_V_DIM
DIFF_ROT = DIFF_HEAD_DIM // 4
SUBLN_EPS = 1e-5

MLA_HEADS = 6
MLA_NOPE = 128
MLA_ROPE = 64
MLA_V = 128
MLA_Q_RANK = 512
MLA_KV_RANK = 256
MLA_OUT = MLA_HEADS * MLA_V

MIX_WIDTH = LRU_WIDTH + DIFF_OUT + MLA_OUT
IN_SIZES = (LRU_WIDTH, LRU_WIDTH, DIFF_QK, DIFF_QK, DIFF_OUT, MLA_Q_RANK, MLA_KV_RANK + MLA_ROPE)
IN_WIDTH = 768 + 768 + 512 + 512 + 512 + 512 + 320
IN_SPLITS = (768, 1536, 2048, 2560, 3072, 3584)

D_FF = -(-8 * D_MODEL // (3 * 256)) * 256

kernel_name = "hybrid_rglru_diffattn_mla_block"


def rmsnorm(x, g, eps=NORM_EPS):
    xf = x.astype(jnp.float32)
    y = xf * lax.rsqrt(jnp.mean(xf * xf, axis=-1, keepdims=True) + eps)
    return (y * g.astype(jnp.float32)).astype(x.dtype)


def rope(x, positions, rot_dim):
    half = rot_dim // 2
    inv_freq = ROPE_THETA ** (-jnp.arange(half, dtype=jnp.float32) / half)
    ang = positions.astype(jnp.float32)[..., None] * inv_freq
    ang = ang.reshape(ang.shape[:2] + (1,) * (x.ndim - 3) + (half,))
    cos = jnp.cos(ang).astype(x.dtype)
    sin = jnp.sin(ang).astype(x.dtype)
    x1 = x[..., :half]
    x2 = x[..., half:rot_dim]
    return jnp.concatenate([x1 * cos - x2 * sin, x2 * cos + x1 * sin, x[..., rot_dim:]], axis=-1)


def chunk_causal_attention(q, k, v, coeff, scale):
    B, S, H, M, d = q.shape
    dv = v.shape[-1]
    n_blk = S // Q_BLOCK
    key_chunk = jnp.arange(S) // CHUNK
    kf = k.astype(jnp.float32)
    vf = v.astype(jnp.float32)
    cf = coeff.astype(jnp.float32)
    q_blocks = jnp.moveaxis(q.reshape(B, n_blk, Q_BLOCK, H, M, d), 1, 0)

    def one_block(args):
        q_blk, blk = args
        s = jnp.einsum('bqhmd,bkhmd->bhmqk', q_blk.astype(jnp.float32), kf) * scale
        q_chunk = (blk * Q_BLOCK + jnp.arange(Q_BLOCK)) // CHUNK
        mask = key_chunk[None, :] <= q_chunk[:, None]
        s = jnp.where(mask, s, -1e30)
        p = jax.nn.softmax(s, axis=-1)
        w = jnp.einsum('bhmqk,m->bhqk', p, cf)
        return jnp.einsum('bhqk,bkhd->bqhd', w, vf)

    out = lax.map(one_block, (q_blocks, jnp.arange(n_blk)))
    return jnp.moveaxis(out, 0, 1).reshape(B, S, H, dv).astype(v.dtype)


def _lru_combine(c1, c2):
    a1, b1 = c1
    a2, b2 = c2
    return a1 * a2, a2 * b1 + b2


def rglru_group(xb, yb, conv_w, conv_b, w_r, b_r, w_i, b_i, lru_lambda):
    B, S, C = xb.shape
    xc = lax.conv_general_dilated(
        xb, conv_w[:, None, :].astype(xb.dtype), window_strides=(1,),
        padding=[(CONV_W - 1, 0)], dimension_numbers=('NWC', 'WIO', 'NWC'),
        feature_group_count=C) + conv_b
    xh = xc.reshape(B, S, LRU_BLOCKS, LRU_BLOCK_W)
    r = jax.nn.sigmoid(jnp.einsum('bshc,hcd->bshd', xh, w_r).reshape(B, S, C) + b_r)
    i = jax.nn.sigmoid(jnp.einsum('bshc,hcd->bshd', xh, w_i).reshape(B, S, C) + b_i)
    log_a = -LRU_C * r.astype(jnp.float32) * jax.nn.softplus(-lru_lambda.astype(jnp.float32))
    a = jnp.exp(log_a)
    b = jnp.sqrt(-jnp.expm1(2.0 * log_a)) * (i * xc).astype(jnp.float32)
    _, h = lax.associative_scan(_lru_combine, (a, b), axis=1)
    return h.astype(xb.dtype) * jax.nn.gelu(yb)


def diff_attention_group(dq, dk, dv, positions, lam_q1, lam_k1, lam_q2, lam_k2, g_sub, lambda_init):
    B, S, _ = dq.shape
    q = rope(dq.reshape(B, S, DIFF_HEADS, 2, DIFF_HEAD_DIM), positions, DIFF_ROT)
    k = rope(dk.reshape(B, S, DIFF_HEADS, 2, DIFF_HEAD_DIM), positions, DIFF_ROT)
    v = dv.reshape(B, S, DIFF_HEADS, DIFF_V_DIM)
    lam = (jnp.exp(jnp.sum(lam_q1.astype(jnp.float32) * lam_k1.astype(jnp.float32)))
           - jnp.exp(jnp.sum(lam_q2.astype(jnp.float32) * lam_k2.astype(jnp.float32)))
           + lambda_init)
    coeff = jnp.stack([jnp.ones((), jnp.float32), -lam])
    o = chunk_causal_attention(q, k, v, coeff, DIFF_HEAD_DIM ** -0.5)
    o = rmsnorm(o, g_sub, eps=SUBLN_EPS) * (1.0 - lambda_init)
    return o.reshape(B, S, DIFF_OUT)


def mla_group(q_a, kv_a, positions, g_q_a, w_q_b, g_kv_a, w_kv_b):
    B, S, _ = q_a.shape
    q = (rmsnorm(q_a, g_q_a) @ w_q_b).reshape(B, S, MLA_HEADS, MLA_NOPE + MLA_ROPE)
    q = jnp.concatenate([q[..., :MLA_NOPE], rope(q[..., MLA_NOPE:], positions, MLA_ROPE)], axis=-1)
    kv_c = kv_a[..., :MLA_KV_RANK]
    k_rope = rope(kv_a[..., MLA_KV_RANK:], positions, MLA_ROPE)
    kv = (rmsnorm(kv_c, g_kv_a) @ w_kv_b).reshape(B, S, MLA_HEADS, MLA_NOPE + MLA_V)
    k = jnp.concatenate(
        [kv[..., :MLA_NOPE], jnp.broadcast_to(k_rope[:, :, None, :], (B, S, MLA_HEADS, MLA_ROPE))], axis=-1)
    v = kv[..., MLA_NOPE:]
    o = chunk_causal_attention(q[:, :, :, None, :], k[:, :, :, None, :], v,
                               jnp.ones((1,), jnp.float32), (MLA_NOPE + MLA_ROPE) ** -0.5)
    return o.reshape(B, S, MLA_OUT)


def setup_inputs(seed: int = 0) -> dict:
    key = jax.random.key(seed)
    ks = jax.random.split(key, 32)

    def nrm(k, shape, scale):
        return jax.random.normal(k, shape, jnp.float32) * scale

    def gain(k, shape):
        return 1.0 + 0.01 * jax.random.normal(k, shape, jnp.float32)

    x = jax.random.normal(ks[0], (BATCH, SEQ, D_MODEL), jnp.float32)
    offset = jax.random.randint(ks[1], (BATCH, 1), 0, 4096, dtype=jnp.int32)
    positions = offset + jnp.arange(SEQ, dtype=jnp.int32)[None, :]

    a0 = jax.random.uniform(ks[11], (DEPTH, LRU_WIDTH), jnp.float32, 0.9, 0.999)
    s0 = a0 ** (1.0 / LRU_C)
    lru_lambda = jnp.log(s0) - jnp.log1p(-s0)

    return {
        "x": x,
        "positions": positions,
        "g_mix": gain(ks[2], (DEPTH, D_MODEL)),
        "w_in": nrm(ks[3], (DEPTH, D_MODEL, IN_WIDTH), D_MODEL ** -0.5),
        "conv_w": nrm(ks[4], (DEPTH, CONV_W, LRU_WIDTH), CONV_W ** -0.5),
        "conv_b": nrm(ks[5], (DEPTH, LRU_WIDTH), 0.01),
        "w_r": nrm(ks[6], (DEPTH, LRU_BLOCKS, LRU_BLOCK_W, LRU_BLOCK_W), LRU_BLOCK_W ** -0.5),
        "b_r": nrm(ks[7], (DEPTH, LRU_WIDTH), 0.01),
        "w_i": nrm(ks[8], (DEPTH, LRU_BLOCKS, LRU_BLOCK_W, LRU_BLOCK_W), LRU_BLOCK_W ** -0.5),
        "b_i": nrm(ks[9], (DEPTH, LRU_WIDTH), 0.01),
        "lru_lambda": lru_lambda,
        "lam_q1": nrm(ks[12], (DEPTH, DIFF_HEAD_DIM), 0.1),
        "lam_k1": nrm(ks[13], (DEPTH, DIFF_HEAD_DIM), 0.1),
        "lam_q2": nrm(ks[14], (DEPTH, DIFF_HEAD_DIM), 0.1),
        "lam_k2": nrm(ks[15], (DEPTH, DIFF_HEAD_DIM), 0.1),
        "g_sub": gain(ks[16], (DEPTH, DIFF_V_DIM)),
        "g_q_a": gain(ks[17], (DEPTH, MLA_Q_RANK)),
        "w_q_b": nrm(ks[18], (DEPTH, MLA_Q_RANK, MLA_HEADS * (MLA_NOPE + MLA_ROPE)), MLA_Q_RANK ** -0.5),
        "g_kv_a": gain(ks[19], (DEPTH, MLA_KV_RANK)),
        "w_kv_b": nrm(ks[20], (DEPTH, MLA_KV_RANK, MLA_HEADS * (MLA_NOPE + MLA_V)), MLA_KV_RANK ** -0.5),
        "w_out": nrm(ks[21], (DEPTH, MIX_WIDTH, D_MODEL), MIX_WIDTH ** -0.5),
        "g_ffn": gain(ks[22], (DEPTH, D_MODEL)),
        "w_gate": nrm(ks[23], (DEPTH, D_MODEL, D_FF), D_MODEL ** -0.5),
        "w_up": nrm(ks[24], (DEPTH, D_MODEL, D_FF), D_MODEL ** -0.5),
        "w_down": nrm(ks[25], (DEPTH, D_FF, D_MODEL), D_FF ** -0.5),
        "g_final": gain(ks[26], (D_MODEL,)),
    }


def reference(x, positions, g_mix, w_in, conv_w, conv_b, w_r, b_r, w_i, b_i, lru_lambda,
              lam_q1, lam_k1, lam_q2, lam_k2, g_sub, g_q_a, w_q_b, g_kv_a, w_kv_b,
              w_out, g_ffn, w_gate, w_up, w_down, g_final):
    for l in range(DEPTH):
        lambda_init = 0.8 - 0.6 * math.exp(-0.3 * l)
        h = rmsnorm(x, g_mix[l])
        proj = h @ w_in[l]
        lru_x, lru_y, dq, dk, dv, q_a, kv_a = jnp.split(proj, IN_SPLITS, axis=-1)
        out_a = rglru_group(lru_x, lru_y, conv_w[l], conv_b[l], w_r[l], b_r[l], w_i[l], b_i[l], lru_lambda[l])
        out_b = diff_attention_group(dq, dk, dv, positions, lam_q1[l], lam_k1[l], lam_q2[l], lam_k2[l],
                                     g_sub[l], lambda_init)
        out_c = mla_group(q_a, kv_a, positions, g_q_a[l], w_q_b[l], g_kv_a[l], w_kv_b[l])
        mix = jnp.concatenate([out_a, out_b, out_c], axis=-1)
        x = x + mix @ w_out[l]
        h = rmsnorm(x, g_ffn[l])
        x = x + (jax.nn.silu(h @ w_gate[l]) * (h @ w_up[l])) @ w_down[l]
    return rmsnorm(x, g_final)
```

```python
import functools
import math

import jax
import jax.numpy as jnp
import numpy as np
from jax import lax
from jax.experimental import pallas as pl
from jax.experimental.pallas import tpu as pltpu

D_MODEL = 2048
SEQ = 8192
DEPTH = 4
CHUNK = 64
ROPE_THETA = 500000.0
NORM_EPS = 1e-6

LRU_WIDTH = 768
LRU_BLOCKS = 6
LRU_BLOCK_W = 128
CONV_W = 4
LRU_C = 8.0

DIFF_HEADS = 4
DIFF_HEAD_DIM = 64
DIFF_V_DIM = 128
DIFF_QK = 512
DIFF_OUT = 512
DIFF_ROT = 16
SUBLN_EPS = 1e-5

MLA_HEADS = 6
MLA_NOPE = 128
MLA_ROPE = 64
MLA_V = 128
MLA_QK = MLA_NOPE + MLA_ROPE
MLA_Q_RANK = 512
MLA_KV_RANK = 256
MLA_OUT = 768

IN_WIDTH = 3904
IN_WIDTH_PAD = 3968
D_FF = 5632

LANES = 128
SUBLANES = 8
VMEM_LIMIT = 60 * 1024 * 1024

TM_PROJ = 512
TS_LRU = 512
T_ATT = 512
TM_OUT = 512
TM_FFN = 1024
TF_FFN = 512

F32 = jnp.float32
BF16 = jnp.bfloat16


def _params(*sem):
    return pltpu.CompilerParams(dimension_semantics=sem, vmem_limit_bytes=VMEM_LIMIT)


def _resident(shape):
    nd = len(shape)
    return pl.BlockSpec(shape, lambda *_: (0,) * nd, pipeline_mode=pl.Buffered(1))


def _rms(x, g, eps):
    return x * lax.rsqrt(jnp.mean(x * x, axis=-1, keepdims=True) + eps) * g


def _rope_table_kernel(pos_ref, invf_ref, lo_ref, hi_ref, c_ref, sp_ref, sm_ref):
    ang = pos_ref[...].astype(F32) * invf_ref[...]
    lo = lo_ref[...]
    hi = hi_ref[...]
    c_ref[...] = jnp.where(lo + hi > 0.0, jnp.cos(ang), 1.0)
    s = jnp.sin(ang)
    sp_ref[...] = s * hi
    sm_ref[...] = -s * lo


def _rope_tables(pos_col, group, rot_dim):
    half = rot_dim // 2
    inv_freq = ROPE_THETA ** (-jnp.arange(half, dtype=F32) / half)
    lane = np.arange(LANES) % group
    sel = jnp.asarray(lane % half)
    rot = lane < rot_dim
    invf = jnp.where(jnp.asarray(rot), inv_freq[sel], 0.0).reshape(1, LANES)
    lo = jnp.asarray((lane < half).astype(np.float32)).reshape(1, LANES)
    hi = jnp.asarray(((lane >= half) & rot).astype(np.float32)).reshape(1, LANES)
    s = pos_col.shape[0]
    tm = 1024
    row = pl.BlockSpec((1, LANES), lambda i: (0, 0))
    out = pl.BlockSpec((tm, LANES), lambda i: (i, 0))
    return pl.pallas_call(
        _rope_table_kernel,
        grid=(s // tm,),
        in_specs=[pl.BlockSpec((tm, 1), lambda i: (i, 0)), row, row, row],
        out_specs=[out, out, out],
        out_shape=[jax.ShapeDtypeStruct((s, LANES), F32)] * 3,
        compiler_params=_params("parallel"),
        name="rope_tables",
    )(pos_col, invf, lo, hi)


def _rope_cols(x, c, sp, sm, half):
    return x * c + pltpu.roll(x, half, 1) * sp + pltpu.roll(x, LANES - half, 1) * sm


def _inproj_kernel(x_ref, g_ref, win_ref, wqb_ref, wkvb_ref, gqa_ref, gkva_ref,
                   dc_ref, dsp_ref, dsm_ref, mc_ref, msp_ref, msm_ref,
                   lrux_ref, lruy_ref, dq_ref, dk_ref, dv_ref, qm_ref, km_ref, vm_ref):
    hb = _rms(x_ref[...], g_ref[...], NORM_EPS).astype(BF16)

    def proj(lo, hi):
        return jnp.dot(hb, win_ref[:, lo:hi], preferred_element_type=F32)

    lrux_ref[...] = proj(0, 768)
    lruy_ref[...] = proj(768, 1536)

    dc, dsp, dsm = dc_ref[...], dsp_ref[...], dsm_ref[...]
    dq = proj(1536, 2048)
    dk = proj(2048, 2560)
    diff_scale = DIFF_HEAD_DIM ** -0.5
    for h in range(DIFF_HEADS):
        cols = slice(h * LANES, (h + 1) * LANES)
        dq_ref[:, cols] = (_rope_cols(dq[:, cols], dc, dsp, dsm, DIFF_ROT // 2) * diff_scale).astype(BF16)
        dk_ref[:, cols] = _rope_cols(dk[:, cols], dc, dsp, dsm, DIFF_ROT // 2).astype(BF16)
    dv_ref[...] = proj(2560, 3072).astype(BF16)

    mc, msp, msm = mc_ref[...], msp_ref[...], msm_ref[...]
    qa = proj(3072, 3584)
    qn = _rms(qa, gqa_ref[...], NORM_EPS).astype(BF16)
    q = jnp.dot(qn, wqb_ref[...], preferred_element_type=F32)
    mla_scale = MLA_QK ** -0.5
    for h in range(MLA_HEADS):
        base = h * 2 * LANES
        qm_ref[h, :, 0:MLA_NOPE] = (q[:, base:base + LANES] * mla_scale).astype(BF16)
        qr = _rope_cols(q[:, base + LANES:base + 2 * LANES], mc, msp, msm, MLA_ROPE // 2)
        qm_ref[h, :, MLA_NOPE:MLA_QK] = (qr[:, :MLA_ROPE] * mla_scale).astype(BF16)

    kva = proj(3584, IN_WIDTH_PAD)
    kr = _rope_cols(kva[:, MLA_KV_RANK:MLA_KV_RANK + LANES], mc, msp, msm, MLA_ROPE // 2)
    kr = kr[:, :MLA_ROPE].astype(BF16)
    kvn = _rms(kva[:, :MLA_KV_RANK], gkva_ref[...], NORM_EPS).astype(BF16)
    kv = jnp.dot(kvn, wkvb_ref[...], preferred_element_type=F32)
    for h in range(MLA_HEADS):
        base = h * 2 * LANES
        km_ref[h, :, 0:MLA_NOPE] = kv[:, base:base + LANES].astype(BF16)
        km_ref[h, :, MLA_NOPE:MLA_QK] = kr
        vm_ref[:, h * LANES:(h + 1) * LANES] = kv[:, base + LANES:base + 2 * LANES].astype(BF16)


def _inproj(x, g, win, wqb, wkvb, gqa, gkva, dtab, mtab):
    s = x.shape[0]
    tm = TM_PROJ
    rows = lambda w: pl.BlockSpec((tm, w), lambda i: (i, 0))
    heads = pl.BlockSpec((MLA_HEADS, tm, MLA_QK), lambda i: (0, i, 0))
    tab = rows(LANES)
    return pl.pallas_call(
        _inproj_kernel,
        grid=(s // tm,),
        in_specs=[rows(D_MODEL), _resident((1, D_MODEL)), _resident(win.shape), _resident(wqb.shape),
                  _resident(wkvb.shape), _resident((1, MLA_Q_RANK)), _resident((1, MLA_KV_RANK)),
                  tab, tab, tab, tab, tab, tab],
        out_specs=[rows(LRU_WIDTH), rows(LRU_WIDTH), rows(DIFF_QK), rows(DIFF_QK), rows(DIFF_OUT),
                   heads, heads, rows(MLA_OUT)],
        out_shape=[jax.ShapeDtypeStruct((s, LRU_WIDTH), F32), jax.ShapeDtypeStruct((s, LRU_WIDTH), F32),
                   jax.ShapeDtypeStruct((s, DIFF_QK), BF16), jax.ShapeDtypeStruct((s, DIFF_QK), BF16),
                   jax.ShapeDtypeStruct((s, DIFF_OUT), BF16),
                   jax.ShapeDtypeStruct((MLA_HEADS, s, MLA_QK), BF16),
                   jax.ShapeDtypeStruct((MLA_HEADS, s, MLA_QK), BF16),
                   jax.ShapeDtypeStruct((s, MLA_OUT), BF16)],
        compiler_params=_params("parallel"),
        name="inproj",
    )(x, g, win, wqb, wkvb, gqa, gkva, *dtab, *mtab)


def _lru_kernel(x_ref, y_ref, cw_ref, cb_ref, wr_ref, br_ref, wi_ref, bi_ref, lam_ref, o_ref,
                xext_sc, a_sc, b_sc, carry_sc):
    ts = x_ref.shape[0]
    pad = SUBLANES

    @pl.when(pl.program_id(0) == 0)
    def _():
        xext_sc[0:pad, :] = jnp.zeros((pad, LRU_WIDTH), F32)
        carry_sc[...] = jnp.zeros_like(carry_sc)

    xext_sc[pad:pad + ts, :] = x_ref[...]
    xc = cb_ref[...] + cw_ref[CONV_W - 1:CONV_W, :] * xext_sc[pad:pad + ts, :]
    for k in range(1, CONV_W):
        xc = xc + cw_ref[CONV_W - 1 - k:CONV_W - k, :] * xext_sc[pad - k:pad - k + ts, :]
    xext_sc[0:pad, :] = x_ref[ts - pad:ts, :]

    xcb = xc.astype(BF16)
    r_parts, i_parts = [], []
    for h in range(LRU_BLOCKS):
        cols = slice(h * LRU_BLOCK_W, (h + 1) * LRU_BLOCK_W)
        r_parts.append(jnp.dot(xcb[:, cols], wr_ref[h], preferred_element_type=F32))
        i_parts.append(jnp.dot(xcb[:, cols], wi_ref[h], preferred_element_type=F32))
    r = jax.nn.sigmoid(jnp.concatenate(r_parts, axis=1) + br_ref[...])
    ig = jax.nn.sigmoid(jnp.concatenate(i_parts, axis=1) + bi_ref[...])

    z = -lam_ref[...]
    softplus = jnp.maximum(z, 0.0) + jnp.log1p(jnp.exp(-jnp.abs(z)))
    log_a = (-LRU_C) * r * softplus
    a = jnp.exp(log_a)
    b = jnp.sqrt(1.0 - a * a) * (ig * xc)

    row = lax.broadcasted_iota(jnp.int32, (ts, 1), 0) & (SUBLANES - 1)
    d = 1
    while d < SUBLANES:
        keep = row >= d
        b = jnp.where(keep, a * pltpu.roll(b, d, 0) + b, b)
        a = jnp.where(keep, a * pltpu.roll(a, d, 0), a)
        d *= 2
    a_sc[...] = a
    b_sc[...] = b

    def group(gidx, carry):
        rows = pl.ds(pl.multiple_of(gidx * SUBLANES, SUBLANES), SUBLANES)
        h8 = b_sc[rows, :] + a_sc[rows, :] * carry
        b_sc[rows, :] = h8
        return h8[SUBLANES - 1:SUBLANES, :]

    carry_sc[...] = lax.fori_loop(0, ts // SUBLANES, group, carry_sc[...], unroll=8)
    o_ref[...] = (b_sc[...] * jax.nn.gelu(y_ref[...], approximate=True)).astype(BF16)


def _lru(lrux, lruy, cw, cb, wr, br, wi, bi, lam):
    s = lrux.shape[0]
    ts = TS_LRU
    rows = pl.BlockSpec((ts, LRU_WIDTH), lambda i: (i, 0))
    vec = _resident((1, LRU_WIDTH))
    return pl.pallas_call(
        _lru_kernel,
        grid=(s // ts,),
        in_specs=[rows, rows, _resident((CONV_W, LRU_WIDTH)), vec, _resident(wr.shape), vec,
                  _resident(wi.shape), vec, vec],
        out_specs=rows,
        out_shape=jax.ShapeDtypeStruct((s, LRU_WIDTH), BF16),
        scratch_shapes=[pltpu.VMEM((ts + SUBLANES, LRU_WIDTH), F32),
                        pltpu.VMEM((ts, LRU_WIDTH), F32),
                        pltpu.VMEM((ts, LRU_WIDTH), F32),
                        pltpu.VMEM((1, LRU_WIDTH), F32)],
        compiler_params=_params("arbitrary"),
        name="rglru",
    )(lrux, lruy, cw, cb, wr, br, wi, bi, lam)


NEG = -1e30


def _pair_tables(n_blk):
    qi, kj = [], []
    for i in range(n_blk):
        for j in range(i + 1):
            qi.append(i)
            kj.append(j)
    return jnp.asarray(qi, jnp.int32), jnp.asarray(kj, jnp.int32)


def _online_softmax_step(s, v, m_sc, l_sc, acc_sc, idx):
    m_prev = m_sc[idx]
    m_new = jnp.maximum(m_prev, jnp.max(s, axis=1, keepdims=True))
    alpha = jnp.exp(m_prev - m_new)
    p = jnp.exp(s - m_new)
    l_sc[idx] = alpha * l_sc[idx] + jnp.sum(p, axis=1, keepdims=True)
    acc_sc[idx] = alpha * acc_sc[idx] + jnp.dot(p.astype(BF16), v, preferred_element_type=F32)
    m_sc[idx] = m_new


def _qk(q, k):
    return lax.dot_general(q, k, (((1,), (1,)), ((), ())), preferred_element_type=F32)


def _chunk_mask(t):
    shift = CHUNK.bit_length() - 1
    r = lax.broadcasted_iota(jnp.int32, (t, t), 0) >> shift
    c = lax.broadcasted_iota(jnp.int32, (t, t), 1) >> shift
    return c <= r


def _init_stats(m_sc, l_sc, acc_sc):
    m_sc[...] = jnp.full(m_sc.shape, NEG, F32)
    l_sc[...] = jnp.zeros(l_sc.shape, F32)
    acc_sc[...] = jnp.zeros(acc_sc.shape, F32)


def _diff_attn_kernel(qi_ref, kj_ref, q_ref, k_ref, v_ref, lq1_ref, lk1_ref, lq2_ref, lk2_ref,
                      gsub_ref, linit_ref, o_ref, m_sc, l_sc, acc_sc):
    t = pl.program_id(0)
    i = qi_ref[t]
    j = kj_ref[t]

    @pl.when(j == 0)
    def _():
        _init_stats(m_sc, l_sc, acc_sc)

    lane = lax.broadcasted_iota(jnp.int32, (1, LANES), 1)
    first_map = lane < DIFF_HEAD_DIM

    def block(diagonal):
        mask = _chunk_mask(T_ATT) if diagonal else None
        for h in range(DIFF_HEADS):
            cols = slice(h * LANES, (h + 1) * LANES)
            q = q_ref[:, cols]
            k = k_ref[:, cols]
            v = v_ref[:, cols]
            zero = jnp.zeros_like(q)
            for m, qm in enumerate((jnp.where(first_map, q, zero), jnp.where(first_map, zero, q))):
                s = _qk(qm, k)
                if diagonal:
                    s = jnp.where(mask, s, NEG)
                _online_softmax_step(s, v, m_sc, l_sc, acc_sc, 2 * h + m)

    @pl.when(j < i)
    def _():
        block(False)

    @pl.when(j == i)
    def _():
        block(True)
        linit = linit_ref[:, 0:1]
        lam = (jnp.exp(jnp.sum(lq1_ref[...] * lk1_ref[...], axis=1, keepdims=True))
               - jnp.exp(jnp.sum(lq2_ref[...] * lk2_ref[...], axis=1, keepdims=True)) + linit)
        for h in range(DIFF_HEADS):
            o = acc_sc[2 * h] / l_sc[2 * h] - lam * (acc_sc[2 * h + 1] / l_sc[2 * h + 1])
            o = _rms(o, gsub_ref[...], SUBLN_EPS) * (1.0 - linit)
            o_ref[:, h * LANES:(h + 1) * LANES] = o.astype(BF16)


def _diff_attention(dq, dk, dv, lq1, lk1, lq2, lk2, gsub, linit, qi, kj):
    s = dq.shape[0]
    t = T_ATT
    qspec = pl.BlockSpec((t, DIFF_QK), lambda p, qi, kj: (qi[p], 0))
    kspec = pl.BlockSpec((t, DIFF_QK), lambda p, qi, kj: (kj[p], 0))
    small = lambda w: pl.BlockSpec((1, w), lambda p, qi, kj: (0, 0))
    nm = 2 * DIFF_HEADS
    return pl.pallas_call(
        _diff_attn_kernel,
        grid_spec=pltpu.PrefetchScalarGridSpec(
            num_scalar_prefetch=2,
            grid=(qi.shape[0],),
            in_specs=[qspec, kspec, kspec, small(DIFF_HEAD_DIM), small(DIFF_HEAD_DIM),
                      small(DIFF_HEAD_DIM), small(DIFF_HEAD_DIM), small(DIFF_V_DIM), small(LANES)],
            out_specs=qspec,
            scratch_shapes=[pltpu.VMEM((nm, t, 1), F32), pltpu.VMEM((nm, t, 1), F32),
                            pltpu.VMEM((nm, t, DIFF_V_DIM), F32)]),
        out_shape=jax.ShapeDtypeStruct((s, DIFF_OUT), BF16),
        compiler_params=_params("arbitrary"),
        name="diff_attention",
    )(qi, kj, dq, dk, dv, lq1, lk1, lq2, lk2, gsub, linit)


def _mla_attn_kernel(qi_ref, kj_ref, q_ref, k_ref, v_ref, o_ref, m_sc, l_sc, acc_sc):
    t = pl.program_id(0)
    i = qi_ref[t]
    j = kj_ref[t]

    @pl.when(j == 0)
    def _():
        _init_stats(m_sc, l_sc, acc_sc)

    def block(diagonal):
        mask = _chunk_mask(T_ATT) if diagonal else None
        for h in range(MLA_HEADS):
            s = _qk(q_ref[h], k_ref[h])
            if diagonal:
                s = jnp.where(mask, s, NEG)
            _online_softmax_step(s, v_ref[:, h * LANES:(h + 1) * LANES], m_sc, l_sc, acc_sc, h)

    @pl.when(j < i)
    def _():
        block(False)

    @pl.when(j == i)
    def _():
        block(True)
        for h in range(MLA_HEADS):
            o_ref[:, h * LANES:(h + 1) * LANES] = (acc_sc[h] / l_sc[h]).astype(BF16)


def _mla_attention(qm, km, vm, qi, kj):
    s = vm.shape[0]
    t = T_ATT
    qspec = pl.BlockSpec((MLA_HEADS, t, MLA_QK), lambda p, qi, kj: (0, qi[p], 0))
    kspec = pl.BlockSpec((MLA_HEADS, t, MLA_QK), lambda p, qi, kj: (0, kj[p], 0))
    vspec = pl.BlockSpec((t, MLA_OUT), lambda p, qi, kj: (kj[p], 0))
    ospec = pl.BlockSpec((t, MLA_OUT), lambda p, qi, kj: (qi[p], 0))
    return pl.pallas_call(
        _mla_attn_kernel,
        grid_spec=pltpu.PrefetchScalarGridSpec(
            num_scalar_prefetch=2,
            grid=(qi.shape[0],),
            in_specs=[qspec, kspec, vspec],
            out_specs=ospec,
            scratch_shapes=[pltpu.VMEM((MLA_HEADS, t, 1), F32), pltpu.VMEM((MLA_HEADS, t, 1), F32),
                            pltpu.VMEM((MLA_HEADS, t, MLA_V), F32)]),
        out_shape=jax.ShapeDtypeStruct((s, MLA_OUT), BF16),
        compiler_params=_params("arbitrary"),
        name="mla_attention",
    )(qi, kj, qm, km, vm)


def _outproj_kernel(x_ref, a_ref, b_ref, c_ref, w_ref, o_ref):
    acc = jnp.dot(a_ref[...], w_ref[0:LRU_WIDTH, :], preferred_element_type=F32)
    acc = acc + jnp.dot(b_ref[...], w_ref[LRU_WIDTH:LRU_WIDTH + DIFF_OUT, :], preferred_element_type=F32)
    acc = acc + jnp.dot(c_ref[...], w_ref[LRU_WIDTH + DIFF_OUT:D_MODEL, :], preferred_element_type=F32)
    o_ref[...] = x_ref[...] + acc


def _outproj(x, a, b, c, w):
    s = x.shape[0]
    tm = TM_OUT
    rows = lambda wd: pl.BlockSpec((tm, wd), lambda i: (i, 0))
    return pl.pallas_call(
        _outproj_kernel,
        grid=(s // tm,),
        in_specs=[rows(D_MODEL), rows(LRU_WIDTH), rows(DIFF_OUT), rows(MLA_OUT), _resident(w.shape)],
        out_specs=rows(D_MODEL),
        out_shape=jax.ShapeDtypeStruct((s, D_MODEL), F32),
        compiler_params=_params("parallel"),
        name="outproj",
    )(x, a, b, c, w)


def _ffn_kernel(x_ref, g_ref, wg_ref, wu_ref, wd_ref, gf_ref, o_ref, hb_sc, *, final):
    f = pl.program_id(1)

    @pl.when(f == 0)
    def _():
        x = x_ref[...]
        hb_sc[...] = _rms(x, g_ref[...], NORM_EPS).astype(BF16)
        o_ref[...] = x

    hb = hb_sc[...]
    gate = jnp.dot(hb, wg_ref[...], preferred_element_type=F32)
    up = jnp.dot(hb, wu_ref[...], preferred_element_type=F32)
    act = (gate * jax.nn.sigmoid(gate) * up).astype(BF16)
    o_ref[...] += jnp.dot(act, wd_ref[...], preferred_element_type=F32)

    if final:
        @pl.when(f == pl.num_programs(1) - 1)
        def _():
            o_ref[...] = _rms(o_ref[...], gf_ref[...], NORM_EPS)


def _ffn(x, g, wg, wu, wd, gf, final):
    s = x.shape[0]
    tm, tf = TM_FFN, TF_FFN
    rows = pl.BlockSpec((tm, D_MODEL), lambda i, f: (i, 0))
    vec = pl.BlockSpec((1, D_MODEL), lambda i, f: (0, 0))
    return pl.pallas_call(
        functools.partial(_ffn_kernel, final=final),
        grid=(s // tm, D_FF // tf),
        in_specs=[rows, vec, pl.BlockSpec((D_MODEL, tf), lambda i, f: (0, f)),
                  pl.BlockSpec((D_MODEL, tf), lambda i, f: (0, f)),
                  pl.BlockSpec((tf, D_MODEL), lambda i, f: (f, 0)), vec],
        out_specs=rows,
        out_shape=jax.ShapeDtypeStruct((s, D_MODEL), F32),
        scratch_shapes=[pltpu.VMEM((tm, D_MODEL), BF16)],
        compiler_params=_params("parallel", "arbitrary"),
        name="ffn",
    )(x, g, wg, wu, wd, gf)


def kernel(x, positions, g_mix, w_in, conv_w, conv_b, w_r, b_r, w_i, b_i, lru_lambda, lam_q1, lam_k1,
           lam_q2, lam_k2, g_sub, g_q_a, w_q_b, g_kv_a, w_kv_b, w_out, g_ffn, w_gate, w_up, w_down,
           g_final):
    batch, seq, _ = x.shape
    assert batch == 1 and seq % TM_FFN == 0 and seq % T_ATT == 0
    xs = x.reshape(seq, D_MODEL)
    pos_col = positions.reshape(seq, 1)

    dtab = _rope_tables(pos_col, DIFF_HEAD_DIM, DIFF_ROT)
    mtab = _rope_tables(pos_col, LANES, MLA_ROPE)
    qi, kj = _pair_tables(seq // T_ATT)

    win = jnp.pad(w_in, ((0, 0), (0, 0), (0, IN_WIDTH_PAD - IN_WIDTH))).astype(BF16)
    wqb = jnp.pad(w_q_b.reshape(DEPTH, MLA_Q_RANK, MLA_HEADS, MLA_QK),
                  ((0, 0), (0, 0), (0, 0), (0, 2 * LANES - MLA_QK)))
    wqb = wqb.reshape(DEPTH, MLA_Q_RANK, MLA_HEADS * 2 * LANES).astype(BF16)
    wkvb = w_kv_b.astype(BF16)
    wr = w_r.astype(BF16)
    wi = w_i.astype(BF16)
    wout = w_out.astype(BF16)
    wg = w_gate.astype(BF16)
    wu = w_up.astype(BF16)
    wd = w_down.astype(BF16)
    row = lambda v: v.reshape(1, -1)

    for l in range(DEPTH):
        lambda_init = 0.8 - 0.6 * math.exp(-0.3 * l)
        linit = jnp.full((1, LANES), lambda_init, F32)
        lrux, lruy, dq, dk, dv, qm, km, vm = _inproj(
            xs, row(g_mix[l]), win[l], wqb[l], wkvb[l], row(g_q_a[l]), row(g_kv_a[l]), dtab, mtab)
        out_a = _lru(lrux, lruy, conv_w[l], row(conv_b[l]), wr[l], row(b_r[l]), wi[l], row(b_i[l]),
                     row(lru_lambda[l]))
        out_b = _diff_attention(dq, dk, dv, row(lam_q1[l]), row(lam_k1[l]), row(lam_q2[l]),
                                row(lam_k2[l]), row(g_sub[l]), linit, qi, kj)
        out_c = _mla_attention(qm, km, vm, qi, kj)
        xs = _outproj(xs, out_a, out_b, out_c, wout[l])
        xs = _ffn(xs, row(g_ffn[l]), wg[l], wu[l], wd[l], row(g_final), final=(l == DEPTH - 1))
    return xs.reshape(batch, seq, D_MODEL)
```

```python
import functools
import math

import jax
import jax.numpy as jnp
import numpy as np
from jax import lax
from jax.experimental import pallas as pl
from jax.experimental.pallas import tpu as pltpu

D_MODEL = 2048
SEQ = 8192
DEPTH = 4
CHUNK = 64
ROPE_THETA = 500000.0
NORM_EPS = 1e-6

LRU_WIDTH = 768
LRU_BLOCKS = 6
LRU_BLOCK_W = 128
CONV_W = 4
LRU_C = 8.0

DIFF_HEADS = 4
DIFF_HEAD_DIM = 64
DIFF_V_DIM = 128
DIFF_QK = 512
DIFF_OUT = 512
DIFF_ROT = 16
SUBLN_EPS = 1e-5

MLA_HEADS = 6
MLA_NOPE = 128
MLA_ROPE = 64
MLA_V = 128
MLA_QK = MLA_NOPE + MLA_ROPE
MLA_Q_RANK = 512
MLA_KV_RANK = 256
MLA_OUT = 768

IN_WIDTH = 3904
IN_WIDTH_PAD = 3968
D_FF = 5632

LANES = 128
SUBLANES = 8
VMEM_LIMIT = 60 * 1024 * 1024

TM_PROJ = 512
TS_LRU = 512
T_ATT = 512
QK_AHEAD = 1
V_DIM = 128
BF16_ROWS = 16
V_ROWS = V_DIM + BF16_ROWS
TM_OUT = 512
TM_FFN = 1024
TF_FFN = 512

F32 = jnp.float32
BF16 = jnp.bfloat16


def _params(*sem):
    return pltpu.CompilerParams(dimension_semantics=sem, vmem_limit_bytes=VMEM_LIMIT)


def _resident(shape):
    nd = len(shape)
    return pl.BlockSpec(shape, lambda *_: (0,) * nd, pipeline_mode=pl.Buffered(1))


def _layer_resident(stacked, layer):
    nd = stacked.ndim - 1
    return pl.BlockSpec((None,) + stacked.shape[1:], lambda *_: (layer,) + (0,) * nd,
                        pipeline_mode=pl.Buffered(1))


def _rms(x, g, eps):
    return x * lax.rsqrt(jnp.mean(x * x, axis=-1, keepdims=True) + eps) * g


def _rope_table_kernel(pos_ref, invf_ref, lo_ref, hi_ref, c_ref, sp_ref, sm_ref):
    ang = pos_ref[...].astype(F32) * invf_ref[...]
    lo = lo_ref[...]
    hi = hi_ref[...]
    c_ref[...] = jnp.where(lo + hi > 0.0, jnp.cos(ang), 1.0)
    s = jnp.sin(ang)
    sp_ref[...] = s * hi
    sm_ref[...] = -s * lo


def _rope_tables(pos_col, group, rot_dim):
    half = rot_dim // 2
    inv_freq = ROPE_THETA ** (-jnp.arange(half, dtype=F32) / half)
    lane = np.arange(LANES) % group
    sel = jnp.asarray(lane % half)
    rot = lane < rot_dim
    invf = jnp.where(jnp.asarray(rot), inv_freq[sel], 0.0).reshape(1, LANES)
    lo = jnp.asarray((lane < half).astype(np.float32)).reshape(1, LANES)
    hi = jnp.asarray(((lane >= half) & rot).astype(np.float32)).reshape(1, LANES)
    s = pos_col.shape[0]
    tm = 1024
    row = pl.BlockSpec((1, LANES), lambda i: (0, 0))
    out = pl.BlockSpec((tm, LANES), lambda i: (i, 0))
    return pl.pallas_call(
        _rope_table_kernel,
        grid=(s // tm,),
        in_specs=[pl.BlockSpec((tm, 1), lambda i: (i, 0)), row, row, row],
        out_specs=[out, out, out],
        out_shape=[jax.ShapeDtypeStruct((s, LANES), F32)] * 3,
        compiler_params=_params("parallel"),
        name="rope_tables",
    )(pos_col, invf, lo, hi)


def _rope_cols(x, c, sp, sm, half):
    return x * c + pltpu.roll(x, half, 1) * sp + pltpu.roll(x, LANES - half, 1) * sm


def _inproj_kernel(x_ref, g_ref, win_ref, wqb_ref, wkvb_ref, gqa_ref, gkva_ref,
                   dc_ref, dsp_ref, dsm_ref, mc_ref, msp_ref, msm_ref,
                   lrux_ref, lruy_ref, dq_ref, dk_ref, dvt_ref, qm_ref, km_ref, vmt_ref):
    hb = _rms(x_ref[...], g_ref[...], NORM_EPS).astype(BF16)

    def proj(lo, hi):
        return jnp.dot(hb, win_ref[:, lo:hi], preferred_element_type=F32)

    lrux_ref[...] = proj(0, 768)
    lruy_ref[...] = proj(768, 1536)

    dc, dsp, dsm = dc_ref[...], dsp_ref[...], dsm_ref[...]
    dq = proj(1536, 2048)
    dk = proj(2048, 2560)
    diff_scale = DIFF_HEAD_DIM ** -0.5 * LOG2_E
    for h in range(DIFF_HEADS):
        cols = slice(h * LANES, (h + 1) * LANES)
        dq_ref[:, cols] = (_rope_cols(dq[:, cols], dc, dsp, dsm, DIFF_ROT // 2) * diff_scale).astype(BF16)
        dk_ref[:, cols] = _rope_cols(dk[:, cols], dc, dsp, dsm, DIFF_ROT // 2).astype(BF16)
    ones = jnp.ones((BF16_ROWS, x_ref.shape[0]), BF16)
    dv = proj(2560, 3072)
    for h in range(DIFF_HEADS):
        dvt_ref[h, 0:V_DIM, :] = dv[:, h * LANES:(h + 1) * LANES].T.astype(BF16)
        dvt_ref[h, V_DIM:V_ROWS, :] = ones

    mc, msp, msm = mc_ref[...], msp_ref[...], msm_ref[...]
    qa = proj(3072, 3584)
    qn = _rms(qa, gqa_ref[...], NORM_EPS).astype(BF16)
    q = jnp.dot(qn, wqb_ref[...], preferred_element_type=F32)
    mla_scale = MLA_QK ** -0.5 * LOG2_E
    for h in range(MLA_HEADS):
        base = h * 2 * LANES
        qm_ref[h, :, 0:MLA_NOPE] = (q[:, base:base + LANES] * mla_scale).astype(BF16)
        qr = _rope_cols(q[:, base + LANES:base + 2 * LANES], mc, msp, msm, MLA_ROPE // 2)
        qm_ref[h, :, MLA_NOPE:MLA_QK] = (qr[:, :MLA_ROPE] * mla_scale).astype(BF16)

    kva = proj(3584, IN_WIDTH_PAD)
    kr = _rope_cols(kva[:, MLA_KV_RANK:MLA_KV_RANK + LANES], mc, msp, msm, MLA_ROPE // 2)
    kr = kr[:, :MLA_ROPE].astype(BF16)
    kvn = _rms(kva[:, :MLA_KV_RANK], gkva_ref[...], NORM_EPS).astype(BF16)
    kv = jnp.dot(kvn, wkvb_ref[...], preferred_element_type=F32)
    for h in range(MLA_HEADS):
        base = h * 2 * LANES
        km_ref[h, :, 0:MLA_NOPE] = kv[:, base:base + LANES].astype(BF16)
        km_ref[h, :, MLA_NOPE:MLA_QK] = kr
        vmt_ref[h, 0:V_DIM, :] = kv[:, base + LANES:base + 2 * LANES].T.astype(BF16)
        vmt_ref[h, V_DIM:V_ROWS, :] = ones


def _inproj(x, g, win, wqb, wkvb, gqa, gkva, dtab, mtab, layer):
    s = x.shape[0]
    tm = TM_PROJ
    rows = lambda w: pl.BlockSpec((tm, w), lambda i: (i, 0))
    heads = pl.BlockSpec((MLA_HEADS, tm, MLA_QK), lambda i: (0, i, 0))
    vt = lambda nh: pl.BlockSpec((nh, V_ROWS, tm), lambda i: (0, 0, i))
    tab = rows(LANES)
    return pl.pallas_call(
        _inproj_kernel,
        grid=(s // tm,),
        in_specs=[rows(D_MODEL), _resident((1, D_MODEL)), _layer_resident(win, layer),
                  _layer_resident(wqb, layer), _layer_resident(wkvb, layer),
                  _resident((1, MLA_Q_RANK)), _resident((1, MLA_KV_RANK)),
                  tab, tab, tab, tab, tab, tab],
        out_specs=[rows(LRU_WIDTH), rows(LRU_WIDTH), rows(DIFF_QK), rows(DIFF_QK), vt(DIFF_HEADS),
                   heads, heads, vt(MLA_HEADS)],
        out_shape=[jax.ShapeDtypeStruct((s, LRU_WIDTH), F32), jax.ShapeDtypeStruct((s, LRU_WIDTH), F32),
                   jax.ShapeDtypeStruct((s, DIFF_QK), BF16), jax.ShapeDtypeStruct((s, DIFF_QK), BF16),
                   jax.ShapeDtypeStruct((DIFF_HEADS, V_ROWS, s), BF16),
                   jax.ShapeDtypeStruct((MLA_HEADS, s, MLA_QK), BF16),
                   jax.ShapeDtypeStruct((MLA_HEADS, s, MLA_QK), BF16),
                   jax.ShapeDtypeStruct((MLA_HEADS, V_ROWS, s), BF16)],
        compiler_params=_params("parallel"),
        name="inproj",
    )(x, g, win, wqb, wkvb, gqa, gkva, *dtab, *mtab)


def _lru_kernel(x_ref, y_ref, cw_ref, cb_ref, wr_ref, br_ref, wi_ref, bi_ref, lam_ref, o_ref,
                xext_sc, a_sc, b_sc, carry_sc):
    ts = x_ref.shape[0]
    pad = SUBLANES

    @pl.when(pl.program_id(0) == 0)
    def _():
        xext_sc[0:pad, :] = jnp.zeros((pad, LRU_WIDTH), F32)
        carry_sc[...] = jnp.zeros_like(carry_sc)

    xext_sc[pad:pad + ts, :] = x_ref[...]
    xc = cb_ref[...] + cw_ref[CONV_W - 1:CONV_W, :] * xext_sc[pad:pad + ts, :]
    for k in range(1, CONV_W):
        xc = xc + cw_ref[CONV_W - 1 - k:CONV_W - k, :] * xext_sc[pad - k:pad - k + ts, :]
    xext_sc[0:pad, :] = x_ref[ts - pad:ts, :]

    xcb = xc.astype(BF16)
    r_parts, i_parts = [], []
    for h in range(LRU_BLOCKS):
        cols = slice(h * LRU_BLOCK_W, (h + 1) * LRU_BLOCK_W)
        r_parts.append(jnp.dot(xcb[:, cols], wr_ref[h], preferred_element_type=F32))
        i_parts.append(jnp.dot(xcb[:, cols], wi_ref[h], preferred_element_type=F32))
    r = jax.nn.sigmoid(jnp.concatenate(r_parts, axis=1) + br_ref[...])
    ig = jax.nn.sigmoid(jnp.concatenate(i_parts, axis=1) + bi_ref[...])

    z = -lam_ref[...]
    softplus = jnp.maximum(z, 0.0) + jnp.log1p(jnp.exp(-jnp.abs(z)))
    log_a = (-LRU_C) * r * softplus
    a = jnp.exp(log_a)
    t = 1.0 - a * a
    b = jnp.where(t > 0.0, t * lax.rsqrt(t), 0.0) * (ig * xc)

    row = lax.broadcasted_iota(jnp.int32, (ts, 1), 0) & (SUBLANES - 1)
    d = 1
    while d < SUBLANES:
        keep = row >= d
        b = jnp.where(keep, a * pltpu.roll(b, d, 0) + b, b)
        a = jnp.where(keep, a * pltpu.roll(a, d, 0), a)
        d *= 2
    a_sc[...] = a
    b_sc[...] = b

    def group(gidx, carry):
        rows = pl.ds(pl.multiple_of(gidx * SUBLANES, SUBLANES), SUBLANES)
        h8 = b_sc[rows, :] + a_sc[rows, :] * carry
        b_sc[rows, :] = h8
        return h8[SUBLANES - 1:SUBLANES, :]

    carry_sc[...] = lax.fori_loop(0, ts // SUBLANES, group, carry_sc[...], unroll=8)
    o_ref[...] = (b_sc[...] * jax.nn.gelu(y_ref[...], approximate=True)).astype(BF16)


def _lru(lrux, lruy, cw, cb, wr, br, wi, bi, lam, layer):
    s = lrux.shape[0]
    ts = TS_LRU
    rows = pl.BlockSpec((ts, LRU_WIDTH), lambda i: (i, 0))
    vec = _resident((1, LRU_WIDTH))
    return pl.pallas_call(
        _lru_kernel,
        grid=(s // ts,),
        in_specs=[rows, rows, _resident((CONV_W, LRU_WIDTH)), vec, _layer_resident(wr, layer), vec,
                  _layer_resident(wi, layer), vec, vec],
        out_specs=rows,
        out_shape=jax.ShapeDtypeStruct((s, LRU_WIDTH), BF16),
        scratch_shapes=[pltpu.VMEM((ts + SUBLANES, LRU_WIDTH), F32),
                        pltpu.VMEM((ts, LRU_WIDTH), F32),
                        pltpu.VMEM((ts, LRU_WIDTH), F32),
                        pltpu.VMEM((1, LRU_WIDTH), F32)],
        compiler_params=_params("arbitrary"),
        name="rglru",
    )(lrux, lruy, cw, cb, wr, br, wi, bi, lam)


NEG = -1e30
LOG2_E = math.log2(math.e)


def _pair_tables(n_blk):
    qi, kj = [], []
    for i in range(n_blk):
        for j in range(i + 1):
            qi.append(i)
            kj.append(j)
    return jnp.asarray(qi, jnp.int32), jnp.asarray(kj, jnp.int32)


def _online_softmax_step(st, vt1, m_sc, acc_sc, idx):
    m_prev = m_sc[idx]
    m_new = jnp.maximum(m_prev, jnp.max(st, axis=0, keepdims=True))
    alpha = jnp.exp2(m_prev - m_new)
    pt = jnp.exp2((st - m_new).astype(BF16))
    acc_sc[idx] = alpha * acc_sc[idx] + jnp.dot(vt1, pt, preferred_element_type=F32)
    m_sc[idx] = m_new


def _kq(k, q):
    return lax.dot_general(k, q, (((1,), (1,)), ((), ())), preferred_element_type=F32)


def _chunk_mask(t):
    shift = CHUNK.bit_length() - 1
    key = lax.broadcasted_iota(jnp.int32, (t, t), 0) >> shift
    qry = lax.broadcasted_iota(jnp.int32, (t, t), 1) >> shift
    return key <= qry


def _pipelined_maps(scores, values, m_sc, acc_sc, n_maps):
    pending = [scores(idx) for idx in range(min(QK_AHEAD, n_maps))]
    for idx in range(n_maps):
        if idx + QK_AHEAD < n_maps:
            pending.append(scores(idx + QK_AHEAD))
        _online_softmax_step(pending.pop(0), values(idx), m_sc, acc_sc, idx)


def _init_stats(m_sc, acc_sc):
    m_sc[...] = jnp.full(m_sc.shape, NEG, F32)
    acc_sc[...] = jnp.zeros(acc_sc.shape, F32)


def _normalized(acc_sc, idx):
    return acc_sc[idx, 0:V_DIM, :] / acc_sc[idx, V_DIM:V_DIM + 1, :]


def _diff_attn_kernel(qi_ref, kj_ref, q_ref, k_ref, vt_ref, lq1_ref, lk1_ref, lq2_ref, lk2_ref,
                      gsub_ref, linit_ref, o_ref, m_sc, acc_sc):
    t = pl.program_id(0)
    i = qi_ref[t]
    j = kj_ref[t]

    @pl.when(j == 0)
    def _():
        _init_stats(m_sc, acc_sc)

    lane = lax.broadcasted_iota(jnp.int32, (1, LANES), 1)
    first_map = lane < DIFF_HEAD_DIM

    def block(diagonal):
        mask = _chunk_mask(T_ATT) if diagonal else None

        def scores(idx):
            h, m = divmod(idx, 2)
            cols = slice(h * LANES, (h + 1) * LANES)
            q = q_ref[:, cols]
            zero = jnp.zeros_like(q)
            qm = jnp.where(first_map, q, zero) if m == 0 else jnp.where(first_map, zero, q)
            st = _kq(k_ref[:, cols], qm)
            return jnp.where(mask, st, NEG) if diagonal else st

        _pipelined_maps(scores, lambda idx: vt_ref[idx // 2], m_sc, acc_sc, 2 * DIFF_HEADS)

    @pl.when(j < i)
    def _():
        block(False)

    @pl.when(j == i)
    def _():
        block(True)
        linit = linit_ref[:, 0:1]
        lam = (jnp.exp(jnp.sum(lq1_ref[...] * lk1_ref[...], axis=1, keepdims=True))
               - jnp.exp(jnp.sum(lq2_ref[...] * lk2_ref[...], axis=1, keepdims=True)) + linit)
        for h in range(DIFF_HEADS):
            ot = _normalized(acc_sc, 2 * h) - lam * _normalized(acc_sc, 2 * h + 1)
            o = _rms(ot.T, gsub_ref[...], SUBLN_EPS) * (1.0 - linit)
            o_ref[:, h * LANES:(h + 1) * LANES] = o.astype(BF16)


def _diff_attention(dq, dk, dvt, lq1, lk1, lq2, lk2, gsub, linit, qi, kj):
    s = dq.shape[0]
    t = T_ATT
    qspec = pl.BlockSpec((t, DIFF_QK), lambda p, qi, kj: (qi[p], 0))
    kspec = pl.BlockSpec((t, DIFF_QK), lambda p, qi, kj: (kj[p], 0))
    vspec = pl.BlockSpec((DIFF_HEADS, V_ROWS, t), lambda p, qi, kj: (0, 0, kj[p]))
    small = lambda w: pl.BlockSpec((1, w), lambda p, qi, kj: (0, 0))
    nm = 2 * DIFF_HEADS
    return pl.pallas_call(
        _diff_attn_kernel,
        grid_spec=pltpu.PrefetchScalarGridSpec(
            num_scalar_prefetch=2,
            grid=(qi.shape[0],),
            in_specs=[qspec, kspec, vspec, small(DIFF_HEAD_DIM), small(DIFF_HEAD_DIM),
                      small(DIFF_HEAD_DIM), small(DIFF_HEAD_DIM), small(DIFF_V_DIM), small(LANES)],
            out_specs=qspec,
            scratch_shapes=[pltpu.VMEM((nm, 1, t), F32), pltpu.VMEM((nm, V_ROWS, t), F32)]),
        out_shape=jax.ShapeDtypeStruct((s, DIFF_OUT), BF16),
        compiler_params=_params("arbitrary"),
        name="diff_attention",
    )(qi, kj, dq, dk, dvt, lq1, lk1, lq2, lk2, gsub, linit)


def _mla_attn_kernel(qi_ref, kj_ref, q_ref, k_ref, vt_ref, o_ref, m_sc, acc_sc):
    t = pl.program_id(0)
    i = qi_ref[t]
    j = kj_ref[t]

    @pl.when(j == 0)
    def _():
        _init_stats(m_sc, acc_sc)

    def block(diagonal):
        mask = _chunk_mask(T_ATT) if diagonal else None

        def scores(h):
            st = _kq(k_ref[h], q_ref[h])
            return jnp.where(mask, st, NEG) if diagonal else st

        _pipelined_maps(scores, lambda h: vt_ref[h], m_sc, acc_sc, MLA_HEADS)

    @pl.when(j < i)
    def _():
        block(False)

    @pl.when(j == i)
    def _():
        block(True)
        for h in range(MLA_HEADS):
            o_ref[:, h * LANES:(h + 1) * LANES] = _normalized(acc_sc, h).T.astype(BF16)


def _mla_attention(qm, km, vmt, qi, kj):
    s = qm.shape[1]
    t = T_ATT
    qspec = pl.BlockSpec((MLA_HEADS, t, MLA_QK), lambda p, qi, kj: (0, qi[p], 0))
    kspec = pl.BlockSpec((MLA_HEADS, t, MLA_QK), lambda p, qi, kj: (0, kj[p], 0))
    vspec = pl.BlockSpec((MLA_HEADS, V_ROWS, t), lambda p, qi, kj: (0, 0, kj[p]))
    ospec = pl.BlockSpec((t, MLA_OUT), lambda p, qi, kj: (qi[p], 0))
    return pl.pallas_call(
        _mla_attn_kernel,
        grid_spec=pltpu.PrefetchScalarGridSpec(
            num_scalar_prefetch=2,
            grid=(qi.shape[0],),
            in_specs=[qspec, kspec, vspec],
            out_specs=ospec,
            scratch_shapes=[pltpu.VMEM((MLA_HEADS, 1, t), F32), pltpu.VMEM((MLA_HEADS, V_ROWS, t), F32)]),
        out_shape=jax.ShapeDtypeStruct((s, MLA_OUT), BF16),
        compiler_params=_params("arbitrary"),
        name="mla_attention",
    )(qi, kj, qm, km, vmt)


def _outproj_kernel(x_ref, a_ref, b_ref, c_ref, w_ref, o_ref):
    acc = jnp.dot(a_ref[...], w_ref[0:LRU_WIDTH, :], preferred_element_type=F32)
    acc = acc + jnp.dot(b_ref[...], w_ref[LRU_WIDTH:LRU_WIDTH + DIFF_OUT, :], preferred_element_type=F32)
    acc = acc + jnp.dot(c_ref[...], w_ref[LRU_WIDTH + DIFF_OUT:D_MODEL, :], preferred_element_type=F32)
    o_ref[...] = x_ref[...] + acc


def _outproj(x, a, b, c, w, layer):
    s = x.shape[0]
    tm = TM_OUT
    rows = lambda wd: pl.BlockSpec((tm, wd), lambda i: (i, 0))
    return pl.pallas_call(
        _outproj_kernel,
        grid=(s // tm,),
        in_specs=[rows(D_MODEL), rows(LRU_WIDTH), rows(DIFF_OUT), rows(MLA_OUT),
                  _layer_resident(w, layer)],
        out_specs=rows(D_MODEL),
        out_shape=jax.ShapeDtypeStruct((s, D_MODEL), F32),
        compiler_params=_params("parallel"),
        name="outproj",
    )(x, a, b, c, w)


def _ffn_kernel(x_ref, g_ref, wg_ref, wu_ref, wd_ref, gf_ref, o_ref, hb_sc, *, final):
    f = pl.program_id(1)

    @pl.when(f == 0)
    def _():
        x = x_ref[...]
        hb_sc[...] = _rms(x, g_ref[...], NORM_EPS).astype(BF16)
        o_ref[...] = x

    hb = hb_sc[...]
    gate = jnp.dot(hb, wg_ref[...], preferred_element_type=F32)
    up = jnp.dot(hb, wu_ref[...], preferred_element_type=F32)
    act = (gate * jax.nn.sigmoid(gate) * up).astype(BF16)
    o_ref[...] += jnp.dot(act, wd_ref[...], preferred_element_type=F32)

    if final:
        @pl.when(f == pl.num_programs(1) - 1)
        def _():
            o_ref[...] = _rms(o_ref[...], gf_ref[...], NORM_EPS)


def _ffn(x, g, wg, wu, wd, gf, layer, final):
    s = x.shape[0]
    tm, tf = TM_FFN, TF_FFN
    rows = pl.BlockSpec((tm, D_MODEL), lambda i, f: (i, 0))
    vec = pl.BlockSpec((1, D_MODEL), lambda i, f: (0, 0))
    return pl.pallas_call(
        functools.partial(_ffn_kernel, final=final),
        grid=(s // tm, D_FF // tf),
        in_specs=[rows, vec, pl.BlockSpec((None, D_MODEL, tf), lambda i, f: (layer, 0, f)),
                  pl.BlockSpec((None, D_MODEL, tf), lambda i, f: (layer, 0, f)),
                  pl.BlockSpec((None, tf, D_MODEL), lambda i, f: (layer, f, 0)), vec],
        out_specs=rows,
        out_shape=jax.ShapeDtypeStruct((s, D_MODEL), F32),
        scratch_shapes=[pltpu.VMEM((tm, D_MODEL), BF16)],
        compiler_params=_params("parallel", "arbitrary"),
        name="ffn",
    )(x, g, wg, wu, wd, gf)


def kernel(x, positions, g_mix, w_in, conv_w, conv_b, w_r, b_r, w_i, b_i, lru_lambda, lam_q1, lam_k1,
           lam_q2, lam_k2, g_sub, g_q_a, w_q_b, g_kv_a, w_kv_b, w_out, g_ffn, w_gate, w_up, w_down,
           g_final):
    batch, seq, _ = x.shape
    assert batch == 1 and seq % TM_FFN == 0 and seq % T_ATT == 0
    xs = x.reshape(seq, D_MODEL)
    pos_col = positions.reshape(seq, 1)

    dtab = _rope_tables(pos_col, DIFF_HEAD_DIM, DIFF_ROT)
    mtab = _rope_tables(pos_col, LANES, MLA_ROPE)
    qi, kj = _pair_tables(seq // T_ATT)

    win = jnp.pad(w_in, ((0, 0), (0, 0), (0, IN_WIDTH_PAD - IN_WIDTH))).astype(BF16)
    wqb = jnp.pad(w_q_b.reshape(DEPTH, MLA_Q_RANK, MLA_HEADS, MLA_QK),
                  ((0, 0), (0, 0), (0, 0), (0, 2 * LANES - MLA_QK)))
    wqb = wqb.reshape(DEPTH, MLA_Q_RANK, MLA_HEADS * 2 * LANES).astype(BF16)
    wkvb = w_kv_b.astype(BF16)
    wr = w_r.astype(BF16)
    wi = w_i.astype(BF16)
    wout = w_out.astype(BF16)
    wg = w_gate.astype(BF16)
    wu = w_up.astype(BF16)
    wd = w_down.astype(BF16)
    row = lambda v: v.reshape(1, -1)

    for l in range(DEPTH):
        lambda_init = 0.8 - 0.6 * math.exp(-0.3 * l)
        linit = jnp.full((1, LANES), lambda_init, F32)
        lrux, lruy, dq, dk, dvt, qm, km, vmt = _inproj(
            xs, row(g_mix[l]), win, wqb, wkvb, row(g_q_a[l]), row(g_kv_a[l]), dtab, mtab, l)
        out_a = _lru(lrux, lruy, conv_w[l], row(conv_b[l]), wr, row(b_r[l]), wi, row(b_i[l]),
                     row(lru_lambda[l]), l)
        out_b = _diff_attention(dq, dk, dvt, row(lam_q1[l]), row(lam_k1[l]), row(lam_q2[l]),
                                row(lam_k2[l]), row(g_sub[l]), linit, qi, kj)
        out_c = _mla_attention(qm, km, vmt, qi, kj)
        xs = _outproj(xs, out_a, out_b, out_c, wout, l)
        xs = _ffn(xs, row(g_ffn[l]), wg, wu, wd, row(g_final), l, final=(l == DEPTH - 1))
    return xs.reshape(batch, seq, D_MODEL)
```

```python
import functools
import math

import jax
import jax.numpy as jnp
import numpy as np
from jax import lax
from jax.experimental import pallas as pl
from jax.experimental.pallas import tpu as pltpu

D_MODEL = 2048
SEQ = 8192
DEPTH = 4
CHUNK = 64
ROPE_THETA = 500000.0
NORM_EPS = 1e-6

LRU_WIDTH = 768
LRU_BLOCKS = 6
LRU_BLOCK_W = 128
CONV_W = 4
LRU_C = 8.0

DIFF_HEADS = 4
DIFF_HEAD_DIM = 64
DIFF_V_DIM = 128
DIFF_QK = 512
DIFF_OUT = 512
DIFF_ROT = 16
SUBLN_EPS = 1e-5

MLA_HEADS = 6
MLA_NOPE = 128
MLA_ROPE = 64
MLA_V = 128
MLA_QK = MLA_NOPE + MLA_ROPE
MLA_Q_RANK = 512
MLA_KV_RANK = 256
MLA_OUT = 768

IN_WIDTH = 3904
IN_WIDTH_PAD = 3968
D_FF = 5632

LANES = 128
SUBLANES = 8
VMEM_LIMIT = 60 * 1024 * 1024

TM_PROJ = 512
TS_LRU = 512
T_ATT = 1024
QK_AHEAD = 1
V_DIM = 128
BF16_ROWS = 16
V_ROWS = V_DIM + BF16_ROWS
TM_OUT = 512
TM_FFN = 1024
TF_FFN = 512

F32 = jnp.float32
BF16 = jnp.bfloat16


def _params(*sem):
    return pltpu.CompilerParams(dimension_semantics=sem, vmem_limit_bytes=VMEM_LIMIT)


def _resident(shape):
    nd = len(shape)
    return pl.BlockSpec(shape, lambda *_: (0,) * nd, pipeline_mode=pl.Buffered(1))


def _layer_resident(stacked, layer):
    nd = stacked.ndim - 1
    return pl.BlockSpec((None,) + stacked.shape[1:], lambda *_: (layer,) + (0,) * nd,
                        pipeline_mode=pl.Buffered(1))


def _rms(x, g, eps):
    return x * lax.rsqrt(jnp.mean(x * x, axis=-1, keepdims=True) + eps) * g


def _rope_table_kernel(pos_ref, invf_ref, lo_ref, hi_ref, c_ref, sp_ref, sm_ref):
    ang = pos_ref[...].astype(F32) * invf_ref[...]
    lo = lo_ref[...]
    hi = hi_ref[...]
    c_ref[...] = jnp.where(lo + hi > 0.0, jnp.cos(ang), 1.0)
    s = jnp.sin(ang)
    sp_ref[...] = s * hi
    sm_ref[...] = -s * lo


def _rope_tables(pos_col, group, rot_dim):
    half = rot_dim // 2
    inv_freq = ROPE_THETA ** (-jnp.arange(half, dtype=F32) / half)
    lane = np.arange(LANES) % group
    sel = jnp.asarray(lane % half)
    rot = lane < rot_dim
    invf = jnp.where(jnp.asarray(rot), inv_freq[sel], 0.0).reshape(1, LANES)
    lo = jnp.asarray((lane < half).astype(np.float32)).reshape(1, LANES)
    hi = jnp.asarray(((lane >= half) & rot).astype(np.float32)).reshape(1, LANES)
    s = pos_col.shape[0]
    tm = 1024
    row = pl.BlockSpec((1, LANES), lambda i: (0, 0))
    out = pl.BlockSpec((tm, LANES), lambda i: (i, 0))
    return pl.pallas_call(
        _rope_table_kernel,
        grid=(s // tm,),
        in_specs=[pl.BlockSpec((tm, 1), lambda i: (i, 0)), row, row, row],
        out_specs=[out, out, out],
        out_shape=[jax.ShapeDtypeStruct((s, LANES), F32)] * 3,
        compiler_params=_params("parallel"),
        name="rope_tables",
    )(pos_col, invf, lo, hi)


def _rope_cols(x, c, sp, sm, half):
    return x * c + pltpu.roll(x, half, 1) * sp + pltpu.roll(x, LANES - half, 1) * sm


def _inproj_kernel(x_ref, g_ref, win_ref, wqb_ref, wkvb_ref, gqa_ref, gkva_ref,
                   dc_ref, dsp_ref, dsm_ref, mc_ref, msp_ref, msm_ref,
                   lrux_ref, lruy_ref, dq_ref, dk_ref, dvt_ref, qm_ref, km_ref, vmt_ref):
    hb = _rms(x_ref[...], g_ref[...], NORM_EPS).astype(BF16)

    def proj(lo, hi):
        return jnp.dot(hb, win_ref[:, lo:hi], preferred_element_type=F32)

    lrux_ref[...] = proj(0, 768)
    lruy_ref[...] = proj(768, 1536)

    dc, dsp, dsm = dc_ref[...], dsp_ref[...], dsm_ref[...]
    dq = proj(1536, 2048)
    dk = proj(2048, 2560)
    diff_scale = DIFF_HEAD_DIM ** -0.5 * LOG2_E
    for h in range(DIFF_HEADS):
        cols = slice(h * LANES, (h + 1) * LANES)
        dq_ref[:, cols] = (_rope_cols(dq[:, cols], dc, dsp, dsm, DIFF_ROT // 2) * diff_scale).astype(BF16)
        dk_ref[:, cols] = _rope_cols(dk[:, cols], dc, dsp, dsm, DIFF_ROT // 2).astype(BF16)
    ones = jnp.ones((BF16_ROWS, x_ref.shape[0]), BF16)
    dv = proj(2560, 3072)
    for h in range(DIFF_HEADS):
        dvt_ref[h, 0:V_DIM, :] = dv[:, h * LANES:(h + 1) * LANES].T.astype(BF16)
        dvt_ref[h, V_DIM:V_ROWS, :] = ones

    mc, msp, msm = mc_ref[...], msp_ref[...], msm_ref[...]
    qa = proj(3072, 3584)
    qn = _rms(qa, gqa_ref[...], NORM_EPS).astype(BF16)
    q = jnp.dot(qn, wqb_ref[...], preferred_element_type=F32)
    mla_scale = MLA_QK ** -0.5 * LOG2_E
    for h in range(MLA_HEADS):
        base = h * 2 * LANES
        qm_ref[h, :, 0:MLA_NOPE] = (q[:, base:base + LANES] * mla_scale).astype(BF16)
        qr = _rope_cols(q[:, base + LANES:base + 2 * LANES], mc, msp, msm, MLA_ROPE // 2)
        qm_ref[h, :, MLA_NOPE:MLA_QK] = (qr[:, :MLA_ROPE] * mla_scale).astype(BF16)

    kva = proj(3584, IN_WIDTH_PAD)
    kr = _rope_cols(kva[:, MLA_KV_RANK:MLA_KV_RANK + LANES], mc, msp, msm, MLA_ROPE // 2)
    kr = kr[:, :MLA_ROPE].astype(BF16)
    kvn = _rms(kva[:, :MLA_KV_RANK], gkva_ref[...], NORM_EPS).astype(BF16)
    kv = jnp.dot(kvn, wkvb_ref[...], preferred_element_type=F32)
    for h in range(MLA_HEADS):
        base = h * 2 * LANES
        km_ref[h, :, 0:MLA_NOPE] = kv[:, base:base + LANES].astype(BF16)
        km_ref[h, :, MLA_NOPE:MLA_QK] = kr
        vmt_ref[h, 0:V_DIM, :] = kv[:, base + LANES:base + 2 * LANES].T.astype(BF16)
        vmt_ref[h, V_DIM:V_ROWS, :] = ones


def _inproj(x, g, win, wqb, wkvb, gqa, gkva, dtab, mtab, layer):
    s = x.shape[0]
    tm = TM_PROJ
    rows = lambda w: pl.BlockSpec((tm, w), lambda i: (i, 0))
    heads = pl.BlockSpec((MLA_HEADS, tm, MLA_QK), lambda i: (0, i, 0))
    vt = lambda nh: pl.BlockSpec((nh, V_ROWS, tm), lambda i: (0, 0, i))
    tab = rows(LANES)
    return pl.pallas_call(
        _inproj_kernel,
        grid=(s // tm,),
        in_specs=[rows(D_MODEL), _resident((1, D_MODEL)), _layer_resident(win, layer),
                  _layer_resident(wqb, layer), _layer_resident(wkvb, layer),
                  _resident((1, MLA_Q_RANK)), _resident((1, MLA_KV_RANK)),
                  tab, tab, tab, tab, tab, tab],
        out_specs=[rows(LRU_WIDTH), rows(LRU_WIDTH), rows(DIFF_QK), rows(DIFF_QK), vt(DIFF_HEADS),
                   heads, heads, vt(MLA_HEADS)],
        out_shape=[jax.ShapeDtypeStruct((s, LRU_WIDTH), F32), jax.ShapeDtypeStruct((s, LRU_WIDTH), F32),
                   jax.ShapeDtypeStruct((s, DIFF_QK), BF16), jax.ShapeDtypeStruct((s, DIFF_QK), BF16),
                   jax.ShapeDtypeStruct((DIFF_HEADS, V_ROWS, s), BF16),
                   jax.ShapeDtypeStruct((MLA_HEADS, s, MLA_QK), BF16),
                   jax.ShapeDtypeStruct((MLA_HEADS, s, MLA_QK), BF16),
                   jax.ShapeDtypeStruct((MLA_HEADS, V_ROWS, s), BF16)],
        compiler_params=_params("parallel"),
        name="inproj",
    )(x, g, win, wqb, wkvb, gqa, gkva, *dtab, *mtab)


def _lru_kernel(x_ref, y_ref, cw_ref, cb_ref, wr_ref, br_ref, wi_ref, bi_ref, lam_ref, o_ref,
                xext_sc, a_sc, b_sc, carry_sc):
    ts = x_ref.shape[0]
    pad = SUBLANES

    @pl.when(pl.program_id(0) == 0)
    def _():
        xext_sc[0:pad, :] = jnp.zeros((pad, LRU_WIDTH), F32)
        carry_sc[...] = jnp.zeros_like(carry_sc)

    xext_sc[pad:pad + ts, :] = x_ref[...]
    xc = cb_ref[...] + cw_ref[CONV_W - 1:CONV_W, :] * xext_sc[pad:pad + ts, :]
    for k in range(1, CONV_W):
        xc = xc + cw_ref[CONV_W - 1 - k:CONV_W - k, :] * xext_sc[pad - k:pad - k + ts, :]
    xext_sc[0:pad, :] = x_ref[ts - pad:ts, :]

    xcb = xc.astype(BF16)
    r_parts, i_parts = [], []
    for h in range(LRU_BLOCKS):
        cols = slice(h * LRU_BLOCK_W, (h + 1) * LRU_BLOCK_W)
        r_parts.append(jnp.dot(xcb[:, cols], wr_ref[h], preferred_element_type=F32))
        i_parts.append(jnp.dot(xcb[:, cols], wi_ref[h], preferred_element_type=F32))
    r = jax.nn.sigmoid(jnp.concatenate(r_parts, axis=1) + br_ref[...])
    ig = jax.nn.sigmoid(jnp.concatenate(i_parts, axis=1) + bi_ref[...])

    z = -lam_ref[...]
    softplus = jnp.maximum(z, 0.0) + jnp.log1p(jnp.exp(-jnp.abs(z)))
    log_a = (-LRU_C) * r * softplus
    a = jnp.exp(log_a)
    t = 1.0 - a * a
    b = jnp.where(t > 0.0, t * lax.rsqrt(t), 0.0) * (ig * xc)

    row = lax.broadcasted_iota(jnp.int32, (ts, 1), 0) & (SUBLANES - 1)
    d = 1
    while d < SUBLANES:
        keep = row >= d
        b = jnp.where(keep, a * pltpu.roll(b, d, 0) + b, b)
        a = jnp.where(keep, a * pltpu.roll(a, d, 0), a)
        d *= 2
    a_sc[...] = a
    b_sc[...] = b

    def group(gidx, carry):
        rows = pl.ds(pl.multiple_of(gidx * SUBLANES, SUBLANES), SUBLANES)
        h8 = b_sc[rows, :] + a_sc[rows, :] * carry
        b_sc[rows, :] = h8
        return h8[SUBLANES - 1:SUBLANES, :]

    carry_sc[...] = lax.fori_loop(0, ts // SUBLANES, group, carry_sc[...], unroll=8)
    o_ref[...] = (b_sc[...] * jax.nn.gelu(y_ref[...], approximate=True)).astype(BF16)


def _lru(lrux, lruy, cw, cb, wr, br, wi, bi, lam, layer):
    s = lrux.shape[0]
    ts = TS_LRU
    rows = pl.BlockSpec((ts, LRU_WIDTH), lambda i: (i, 0))
    vec = _resident((1, LRU_WIDTH))
    return pl.pallas_call(
        _lru_kernel,
        grid=(s // ts,),
        in_specs=[rows, rows, _resident((CONV_W, LRU_WIDTH)), vec, _layer_resident(wr, layer), vec,
                  _layer_resident(wi, layer), vec, vec],
        out_specs=rows,
        out_shape=jax.ShapeDtypeStruct((s, LRU_WIDTH), BF16),
        scratch_shapes=[pltpu.VMEM((ts + SUBLANES, LRU_WIDTH), F32),
                        pltpu.VMEM((ts, LRU_WIDTH), F32),
                        pltpu.VMEM((ts, LRU_WIDTH), F32),
                        pltpu.VMEM((1, LRU_WIDTH), F32)],
        compiler_params=_params("arbitrary"),
        name="rglru",
    )(lrux, lruy, cw, cb, wr, br, wi, bi, lam)


NEG = -1e30
LOG2_E = math.log2(math.e)


def _pair_tables(n_blk):
    qi, kj = [], []
    for i in range(n_blk):
        for j in range(i + 1):
            qi.append(i)
            kj.append(j)
    return jnp.asarray(qi, jnp.int32), jnp.asarray(kj, jnp.int32)


def _online_softmax_step(st, vt1, m_sc, acc_sc, idx, qs):
    m_prev = m_sc[idx, :, qs]
    m_new = jnp.maximum(m_prev, jnp.max(st, axis=0, keepdims=True))
    alpha = jnp.exp2(m_prev - m_new)
    pt = jnp.exp2((st - m_new).astype(BF16))
    acc_sc[idx, :, qs] = alpha * acc_sc[idx, :, qs] + jnp.dot(vt1, pt, preferred_element_type=F32)
    m_sc[idx, :, qs] = m_new


def _block_parts(diagonal):
    full, half = slice(0, T_ATT), T_ATT // 2
    if not diagonal:
        return [(full, full)]
    return [(slice(0, half), full), (slice(half, T_ATT), slice(half, T_ATT))]


def _kq(k, q):
    return lax.dot_general(k, q, (((1,), (1,)), ((), ())), preferred_element_type=F32)


def _chunk_mask(t):
    shift = CHUNK.bit_length() - 1
    key = lax.broadcasted_iota(jnp.int32, (t, t), 0) >> shift
    qry = lax.broadcasted_iota(jnp.int32, (t, t), 1) >> shift
    return key <= qry


def _pipelined_maps(scores, values, m_sc, acc_sc, n_maps, diagonal):
    steps = [(idx, ks, qs) for ks, qs in _block_parts(diagonal) for idx in range(n_maps)]
    pending = [scores(*step) for step in steps[:QK_AHEAD]]
    for n, (idx, ks, qs) in enumerate(steps):
        if n + QK_AHEAD < len(steps):
            pending.append(scores(*steps[n + QK_AHEAD]))
        _online_softmax_step(pending.pop(0), values(idx, ks), m_sc, acc_sc, idx, qs)


def _init_stats(m_sc, acc_sc):
    m_sc[...] = jnp.full(m_sc.shape, NEG, F32)
    acc_sc[...] = jnp.zeros(acc_sc.shape, F32)


def _normalized(acc_sc, idx):
    return acc_sc[idx, 0:V_DIM, :] / acc_sc[idx, V_DIM:V_DIM + 1, :]


def _diff_attn_kernel(qi_ref, kj_ref, q_ref, k_ref, vt_ref, lq1_ref, lk1_ref, lq2_ref, lk2_ref,
                      gsub_ref, linit_ref, o_ref, m_sc, acc_sc):
    t = pl.program_id(0)
    i = qi_ref[t]
    j = kj_ref[t]

    @pl.when(j == 0)
    def _():
        _init_stats(m_sc, acc_sc)

    lane = lax.broadcasted_iota(jnp.int32, (1, LANES), 1)
    first_map = lane < DIFF_HEAD_DIM

    def block(diagonal):
        mask = _chunk_mask(T_ATT) if diagonal else None

        def scores(idx, ks, qs):
            h, m = divmod(idx, 2)
            cols = slice(h * LANES, (h + 1) * LANES)
            q = q_ref[qs, cols]
            zero = jnp.zeros_like(q)
            qm = jnp.where(first_map, q, zero) if m == 0 else jnp.where(first_map, zero, q)
            st = _kq(k_ref[ks, cols], qm)
            return jnp.where(mask[ks, qs], st, NEG) if diagonal else st

        _pipelined_maps(scores, lambda idx, ks: vt_ref[idx // 2, :, ks], m_sc, acc_sc,
                        2 * DIFF_HEADS, diagonal)

    @pl.when(j < i)
    def _():
        block(False)

    @pl.when(j == i)
    def _():
        block(True)
        linit = linit_ref[:, 0:1]
        lam = (jnp.exp(jnp.sum(lq1_ref[...] * lk1_ref[...], axis=1, keepdims=True))
               - jnp.exp(jnp.sum(lq2_ref[...] * lk2_ref[...], axis=1, keepdims=True)) + linit)
        for h in range(DIFF_HEADS):
            ot = _normalized(acc_sc, 2 * h) - lam * _normalized(acc_sc, 2 * h + 1)
            o = _rms(ot.T, gsub_ref[...], SUBLN_EPS) * (1.0 - linit)
            o_ref[:, h * LANES:(h + 1) * LANES] = o.astype(BF16)


def _diff_attention(dq, dk, dvt, lq1, lk1, lq2, lk2, gsub, linit, qi, kj):
    s = dq.shape[0]
    t = T_ATT
    qspec = pl.BlockSpec((t, DIFF_QK), lambda p, qi, kj: (qi[p], 0))
    kspec = pl.BlockSpec((t, DIFF_QK), lambda p, qi, kj: (kj[p], 0))
    vspec = pl.BlockSpec((DIFF_HEADS, V_ROWS, t), lambda p, qi, kj: (0, 0, kj[p]))
    small = lambda w: pl.BlockSpec((1, w), lambda p, qi, kj: (0, 0))
    nm = 2 * DIFF_HEADS
    return pl.pallas_call(
        _diff_attn_kernel,
        grid_spec=pltpu.PrefetchScalarGridSpec(
            num_scalar_prefetch=2,
            grid=(qi.shape[0],),
            in_specs=[qspec, kspec, vspec, small(DIFF_HEAD_DIM), small(DIFF_HEAD_DIM),
                      small(DIFF_HEAD_DIM), small(DIFF_HEAD_DIM), small(DIFF_V_DIM), small(LANES)],
            out_specs=qspec,
            scratch_shapes=[pltpu.VMEM((nm, 1, t), F32), pltpu.VMEM((nm, V_ROWS, t), F32)]),
        out_shape=jax.ShapeDtypeStruct((s, DIFF_OUT), BF16),
        compiler_params=_params("arbitrary"),
        name="diff_attention",
    )(qi, kj, dq, dk, dvt, lq1, lk1, lq2, lk2, gsub, linit)


def _mla_attn_kernel(qi_ref, kj_ref, q_ref, k_ref, vt_ref, o_ref, m_sc, acc_sc):
    t = pl.program_id(0)
    i = qi_ref[t]
    j = kj_ref[t]

    @pl.when(j == 0)
    def _():
        _init_stats(m_sc, acc_sc)

    def block(diagonal):
        mask = _chunk_mask(T_ATT) if diagonal else None

        def scores(h, ks, qs):
            st = _kq(k_ref[h, ks, :], q_ref[h, qs, :])
            return jnp.where(mask[ks, qs], st, NEG) if diagonal else st

        _pipelined_maps(scores, lambda h, ks: vt_ref[h, :, ks], m_sc, acc_sc, MLA_HEADS, diagonal)

    @pl.when(j < i)
    def _():
        block(False)

    @pl.when(j == i)
    def _():
        block(True)
        for h in range(MLA_HEADS):
            o_ref[:, h * LANES:(h + 1) * LANES] = _normalized(acc_sc, h).T.astype(BF16)


def _mla_attention(qm, km, vmt, qi, kj):
    s = qm.shape[1]
    t = T_ATT
    qspec = pl.BlockSpec((MLA_HEADS, t, MLA_QK), lambda p, qi, kj: (0, qi[p], 0))
    kspec = pl.BlockSpec((MLA_HEADS, t, MLA_QK), lambda p, qi, kj: (0, kj[p], 0))
    vspec = pl.BlockSpec((MLA_HEADS, V_ROWS, t), lambda p, qi, kj: (0, 0, kj[p]))
    ospec = pl.BlockSpec((t, MLA_OUT), lambda p, qi, kj: (qi[p], 0))
    return pl.pallas_call(
        _mla_attn_kernel,
        grid_spec=pltpu.PrefetchScalarGridSpec(
            num_scalar_prefetch=2,
            grid=(qi.shape[0],),
            in_specs=[qspec, kspec, vspec],
            out_specs=ospec,
            scratch_shapes=[pltpu.VMEM((MLA_HEADS, 1, t), F32), pltpu.VMEM((MLA_HEADS, V_ROWS, t), F32)]),
        out_shape=jax.ShapeDtypeStruct((s, MLA_OUT), BF16),
        compiler_params=_params("arbitrary"),
        name="mla_attention",
    )(qi, kj, qm, km, vmt)


def _outproj_kernel(x_ref, a_ref, b_ref, c_ref, w_ref, o_ref):
    acc = jnp.dot(a_ref[...], w_ref[0:LRU_WIDTH, :], preferred_element_type=F32)
    acc = acc + jnp.dot(b_ref[...], w_ref[LRU_WIDTH:LRU_WIDTH + DIFF_OUT, :], preferred_element_type=F32)
    acc = acc + jnp.dot(c_ref[...], w_ref[LRU_WIDTH + DIFF_OUT:D_MODEL, :], preferred_element_type=F32)
    o_ref[...] = x_ref[...] + acc


def _outproj(x, a, b, c, w, layer):
    s = x.shape[0]
    tm = TM_OUT
    rows = lambda wd: pl.BlockSpec((tm, wd), lambda i: (i, 0))
    return pl.pallas_call(
        _outproj_kernel,
        grid=(s // tm,),
        in_specs=[rows(D_MODEL), rows(LRU_WIDTH), rows(DIFF_OUT), rows(MLA_OUT),
                  _layer_resident(w, layer)],
        out_specs=rows(D_MODEL),
        out_shape=jax.ShapeDtypeStruct((s, D_MODEL), F32),
        compiler_params=_params("parallel"),
        name="outproj",
    )(x, a, b, c, w)


def _ffn_kernel(x_ref, g_ref, wg_ref, wu_ref, wd_ref, gf_ref, o_ref, hb_sc, *, final):
    f = pl.program_id(1)

    @pl.when(f == 0)
    def _():
        x = x_ref[...]
        hb_sc[...] = _rms(x, g_ref[...], NORM_EPS).astype(BF16)
        o_ref[...] = x

    hb = hb_sc[...]
    gate = jnp.dot(hb, wg_ref[...], preferred_element_type=F32)
    up = jnp.dot(hb, wu_ref[...], preferred_element_type=F32)
    act = (gate * jax.nn.sigmoid(gate) * up).astype(BF16)
    o_ref[...] += jnp.dot(act, wd_ref[...], preferred_element_type=F32)

    if final:
        @pl.when(f == pl.num_programs(1) - 1)
        def _():
            o_ref[...] = _rms(o_ref[...], gf_ref[...], NORM_EPS)


def _ffn(x, g, wg, wu, wd, gf, layer, final):
    s = x.shape[0]
    tm, tf = TM_FFN, TF_FFN
    rows = pl.BlockSpec((tm, D_MODEL), lambda i, f: (i, 0))
    vec = pl.BlockSpec((1, D_MODEL), lambda i, f: (0, 0))
    return pl.pallas_call(
        functools.partial(_ffn_kernel, final=final),
        grid=(s // tm, D_FF // tf),
        in_specs=[rows, vec, pl.BlockSpec((None, D_MODEL, tf), lambda i, f: (layer, 0, f)),
                  pl.BlockSpec((None, D_MODEL, tf), lambda i, f: (layer, 0, f)),
                  pl.BlockSpec((None, tf, D_MODEL), lambda i, f: (layer, f, 0)), vec],
        out_specs=rows,
        out_shape=jax.ShapeDtypeStruct((s, D_MODEL), F32),
        scratch_shapes=[pltpu.VMEM((tm, D_MODEL), BF16)],
        compiler_params=_params("parallel", "arbitrary"),
        name="ffn",
    )(x, g, wg, wu, wd, gf)


def kernel(x, positions, g_mix, w_in, conv_w, conv_b, w_r, b_r, w_i, b_i, lru_lambda, lam_q1, lam_k1,
           lam_q2, lam_k2, g_sub, g_q_a, w_q_b, g_kv_a, w_kv_b, w_out, g_ffn, w_gate, w_up, w_down,
           g_final):
    batch, seq, _ = x.shape
    assert batch == 1 and seq % TM_FFN == 0 and seq % T_ATT == 0
    xs = x.reshape(seq, D_MODEL)
    pos_col = positions.reshape(seq, 1)

    dtab = _rope_tables(pos_col, DIFF_HEAD_DIM, DIFF_ROT)
    mtab = _rope_tables(pos_col, LANES, MLA_ROPE)
    qi, kj = _pair_tables(seq // T_ATT)

    win = jnp.pad(w_in, ((0, 0), (0, 0), (0, IN_WIDTH_PAD - IN_WIDTH))).astype(BF16)
    wqb = jnp.pad(w_q_b.reshape(DEPTH, MLA_Q_RANK, MLA_HEADS, MLA_QK),
                  ((0, 0), (0, 0), (0, 0), (0, 2 * LANES - MLA_QK)))
    wqb = wqb.reshape(DEPTH, MLA_Q_RANK, MLA_HEADS * 2 * LANES).astype(BF16)
    wkvb = w_kv_b.astype(BF16)
    wr = w_r.astype(BF16)
    wi = w_i.astype(BF16)
    wout = w_out.astype(BF16)
    wg = w_gate.astype(BF16)
    wu = w_up.astype(BF16)
    wd = w_down.astype(BF16)
    row = lambda v: v.reshape(1, -1)

    for l in range(DEPTH):
        lambda_init = 0.8 - 0.6 * math.exp(-0.3 * l)
        linit = jnp.full((1, LANES), lambda_init, F32)
        lrux, lruy, dq, dk, dvt, qm, km, vmt = _inproj(
            xs, row(g_mix[l]), win, wqb, wkvb, row(g_q_a[l]), row(g_kv_a[l]), dtab, mtab, l)
        out_a = _lru(lrux, lruy, conv_w[l], row(conv_b[l]), wr, row(b_r[l]), wi, row(b_i[l]),
                     row(lru_lambda[l]), l)
        out_b = _diff_attention(dq, dk, dvt, row(lam_q1[l]), row(lam_k1[l]), row(lam_q2[l]),
                                row(lam_k2[l]), row(g_sub[l]), linit, qi, kj)
        out_c = _mla_attention(qm, km, vmt, qi, kj)
        xs = _outproj(xs, out_a, out_b, out_c, wout, l)
        xs = _ffn(xs, row(g_ffn[l]), wg, wu, wd, row(g_final), l, final=(l == DEPTH - 1))
    return xs.reshape(batch, seq, D_MODEL)
```

```python
import functools
import math

import jax
import jax.numpy as jnp
import numpy as np
from jax import lax
from jax.experimental import pallas as pl
from jax.experimental.pallas import tpu as pltpu

D_MODEL = 2048
SEQ = 8192
DEPTH = 4
CHUNK = 64
ROPE_THETA = 500000.0
NORM_EPS = 1e-6

LRU_WIDTH = 768
LRU_BLOCKS = 6
LRU_BLOCK_W = 128
CONV_W = 4
LRU_C = 8.0

DIFF_HEADS = 4
DIFF_HEAD_DIM = 64
DIFF_V_DIM = 128
DIFF_QK = 512
DIFF_OUT = 512
DIFF_ROT = 16
SUBLN_EPS = 1e-5

MLA_HEADS = 6
MLA_NOPE = 128
MLA_ROPE = 64
MLA_V = 128
MLA_QK = MLA_NOPE + MLA_ROPE
MLA_Q_RANK = 512
MLA_KV_RANK = 256
MLA_OUT = 768

IN_WIDTH = 3904
IN_WIDTH_PAD = 3968
D_FF = 5632

LANES = 128
SUBLANES = 8
VMEM_LIMIT = 60 * 1024 * 1024

TM_PROJ = 512
TS_LRU = 512
T_ATT = 1024
QK_AHEAD = 1
V_DIM = 128
BF16_ROWS = 16
V_ROWS = V_DIM + BF16_ROWS
TM_OUT = 512
TM_FFN = 1024
TF_FFN = 512

F32 = jnp.float32
BF16 = jnp.bfloat16


def _params(*sem):
    return pltpu.CompilerParams(dimension_semantics=sem, vmem_limit_bytes=VMEM_LIMIT)


def _resident(shape):
    nd = len(shape)
    return pl.BlockSpec(shape, lambda *_: (0,) * nd, pipeline_mode=pl.Buffered(1))


def _layer_resident(stacked, layer):
    nd = stacked.ndim - 1
    return pl.BlockSpec((None,) + stacked.shape[1:], lambda *_: (layer,) + (0,) * nd,
                        pipeline_mode=pl.Buffered(1))


def _rms(x, g, eps):
    return x * lax.rsqrt(jnp.mean(x * x, axis=-1, keepdims=True) + eps) * g


def _rope_table_kernel(pos_ref, invf_ref, lo_ref, hi_ref, c_ref, sp_ref, sm_ref):
    ang = pos_ref[...].astype(F32) * invf_ref[...]
    lo = lo_ref[...]
    hi = hi_ref[...]
    c_ref[...] = jnp.where(lo + hi > 0.0, jnp.cos(ang), 1.0)
    s = jnp.sin(ang)
    sp_ref[...] = s * hi
    sm_ref[...] = -s * lo


def _rope_tables(pos_col, group, rot_dim):
    half = rot_dim // 2
    inv_freq = ROPE_THETA ** (-jnp.arange(half, dtype=F32) / half)
    lane = np.arange(LANES) % group
    sel = jnp.asarray(lane % half)
    rot = lane < rot_dim
    invf = jnp.where(jnp.asarray(rot), inv_freq[sel], 0.0).reshape(1, LANES)
    lo = jnp.asarray((lane < half).astype(np.float32)).reshape(1, LANES)
    hi = jnp.asarray(((lane >= half) & rot).astype(np.float32)).reshape(1, LANES)
    s = pos_col.shape[0]
    tm = 1024
    row = pl.BlockSpec((1, LANES), lambda i: (0, 0))
    out = pl.BlockSpec((tm, LANES), lambda i: (i, 0))
    return pl.pallas_call(
        _rope_table_kernel,
        grid=(s // tm,),
        in_specs=[pl.BlockSpec((tm, 1), lambda i: (i, 0)), row, row, row],
        out_specs=[out, out, out],
        out_shape=[jax.ShapeDtypeStruct((s, LANES), F32)] * 3,
        compiler_params=_params("parallel"),
        name="rope_tables",
    )(pos_col, invf, lo, hi)


def _rope_cols(x, c, sp, sm, half):
    return x * c + pltpu.roll(x, half, 1) * sp + pltpu.roll(x, LANES - half, 1) * sm


def _inproj_kernel(x_ref, g_ref, win_ref, wqb_ref, wkvb_ref, gqa_ref, gkva_ref,
                   dc_ref, dsp_ref, dsm_ref, mc_ref, msp_ref, msm_ref,
                   lrux_ref, lruy_ref, dq_ref, dk_ref, dvt_ref, qm_ref, km_ref, vmt_ref):
    hb = _rms(x_ref[...], g_ref[...], NORM_EPS).astype(BF16)

    def proj(lo, hi):
        return jnp.dot(hb, win_ref[:, lo:hi], preferred_element_type=F32)

    lrux_ref[...] = proj(0, 768)
    lruy_ref[...] = proj(768, 1536)

    dc, dsp, dsm = dc_ref[...], dsp_ref[...], dsm_ref[...]
    dq = proj(1536, 2048)
    dk = proj(2048, 2560)
    diff_scale = DIFF_HEAD_DIM ** -0.5 * LOG2_E
    for h in range(DIFF_HEADS):
        cols = slice(h * LANES, (h + 1) * LANES)
        dq_ref[:, cols] = (_rope_cols(dq[:, cols], dc, dsp, dsm, DIFF_ROT // 2) * diff_scale).astype(BF16)
        dk_ref[:, cols] = _rope_cols(dk[:, cols], dc, dsp, dsm, DIFF_ROT // 2).astype(BF16)
    ones = jnp.ones((BF16_ROWS, x_ref.shape[0]), BF16)
    dv = proj(2560, 3072)
    for h in range(DIFF_HEADS):
        dvt_ref[h, 0:V_DIM, :] = dv[:, h * LANES:(h + 1) * LANES].T.astype(BF16)
        dvt_ref[h, V_DIM:V_ROWS, :] = ones

    mc, msp, msm = mc_ref[...], msp_ref[...], msm_ref[...]
    qa = proj(3072, 3584)
    qn = _rms(qa, gqa_ref[...], NORM_EPS).astype(BF16)
    q = jnp.dot(qn, wqb_ref[...], preferred_element_type=F32)
    mla_scale = MLA_QK ** -0.5 * LOG2_E
    for h in range(MLA_HEADS):
        base = h * 2 * LANES
        qm_ref[h, :, 0:MLA_NOPE] = (q[:, base:base + LANES] * mla_scale).astype(BF16)
        qr = _rope_cols(q[:, base + LANES:base + 2 * LANES], mc, msp, msm, MLA_ROPE // 2)
        qm_ref[h, :, MLA_NOPE:MLA_QK] = (qr[:, :MLA_ROPE] * mla_scale).astype(BF16)

    kva = proj(3584, IN_WIDTH_PAD)
    kr = _rope_cols(kva[:, MLA_KV_RANK:MLA_KV_RANK + LANES], mc, msp, msm, MLA_ROPE // 2)
    kr = kr[:, :MLA_ROPE].astype(BF16)
    kvn = _rms(kva[:, :MLA_KV_RANK], gkva_ref[...], NORM_EPS).astype(BF16)
    kv = jnp.dot(kvn, wkvb_ref[...], preferred_element_type=F32)
    for h in range(MLA_HEADS):
        base = h * 2 * LANES
        km_ref[h, :, 0:MLA_NOPE] = kv[:, base:base + LANES].astype(BF16)
        km_ref[h, :, MLA_NOPE:MLA_QK] = kr
        vmt_ref[h, 0:V_DIM, :] = kv[:, base + LANES:base + 2 * LANES].T.astype(BF16)
        vmt_ref[h, V_DIM:V_ROWS, :] = ones


def _inproj(x, g, win, wqb, wkvb, gqa, gkva, dtab, mtab, layer):
    s = x.shape[0]
    tm = TM_PROJ
    rows = lambda w: pl.BlockSpec((tm, w), lambda i: (i, 0))
    heads = pl.BlockSpec((MLA_HEADS, tm, MLA_QK), lambda i: (0, i, 0))
    vt = lambda nh: pl.BlockSpec((nh, V_ROWS, tm), lambda i: (0, 0, i))
    tab = rows(LANES)
    return pl.pallas_call(
        _inproj_kernel,
        grid=(s // tm,),
        in_specs=[rows(D_MODEL), _resident((1, D_MODEL)), _layer_resident(win, layer),
                  _layer_resident(wqb, layer), _layer_resident(wkvb, layer),
                  _resident((1, MLA_Q_RANK)), _resident((1, MLA_KV_RANK)),
                  tab, tab, tab, tab, tab, tab],
        out_specs=[rows(LRU_WIDTH), rows(LRU_WIDTH), rows(DIFF_QK), rows(DIFF_QK), vt(DIFF_HEADS),
                   heads, heads, vt(MLA_HEADS)],
        out_shape=[jax.ShapeDtypeStruct((s, LRU_WIDTH), F32), jax.ShapeDtypeStruct((s, LRU_WIDTH), F32),
                   jax.ShapeDtypeStruct((s, DIFF_QK), BF16), jax.ShapeDtypeStruct((s, DIFF_QK), BF16),
                   jax.ShapeDtypeStruct((DIFF_HEADS, V_ROWS, s), BF16),
                   jax.ShapeDtypeStruct((MLA_HEADS, s, MLA_QK), BF16),
                   jax.ShapeDtypeStruct((MLA_HEADS, s, MLA_QK), BF16),
                   jax.ShapeDtypeStruct((MLA_HEADS, V_ROWS, s), BF16)],
        compiler_params=_params("parallel"),
        name="inproj",
    )(x, g, win, wqb, wkvb, gqa, gkva, *dtab, *mtab)


def _lru_kernel(x_ref, y_ref, cw_ref, cb_ref, wr_ref, br_ref, wi_ref, bi_ref, lam_ref, o_ref,
                xext_sc, a_sc, b_sc, carry_sc):
    ts = x_ref.shape[0]
    pad = SUBLANES

    @pl.when(pl.program_id(0) == 0)
    def _():
        xext_sc[0:pad, :] = jnp.zeros((pad, LRU_WIDTH), F32)
        carry_sc[...] = jnp.zeros_like(carry_sc)

    xext_sc[pad:pad + ts, :] = x_ref[...]
    xc = cb_ref[...] + cw_ref[CONV_W - 1:CONV_W, :] * xext_sc[pad:pad + ts, :]
    for k in range(1, CONV_W):
        xc = xc + cw_ref[CONV_W - 1 - k:CONV_W - k, :] * xext_sc[pad - k:pad - k + ts, :]
    xext_sc[0:pad, :] = x_ref[ts - pad:ts, :]

    xcb = xc.astype(BF16)
    r_parts, i_parts = [], []
    for h in range(LRU_BLOCKS):
        cols = slice(h * LRU_BLOCK_W, (h + 1) * LRU_BLOCK_W)
        r_parts.append(jnp.dot(xcb[:, cols], wr_ref[h], preferred_element_type=F32))
        i_parts.append(jnp.dot(xcb[:, cols], wi_ref[h], preferred_element_type=F32))
    r = jax.nn.sigmoid(jnp.concatenate(r_parts, axis=1) + br_ref[...])
    ig = jax.nn.sigmoid(jnp.concatenate(i_parts, axis=1) + bi_ref[...])

    z = -lam_ref[...]
    softplus = jnp.maximum(z, 0.0) + jnp.log1p(jnp.exp(-jnp.abs(z)))
    log_a = (-LRU_C) * r * softplus
    a = jnp.exp(log_a)
    t = 1.0 - a * a
    b = jnp.where(t > 0.0, t * lax.rsqrt(t), 0.0) * (ig * xc)

    row = lax.broadcasted_iota(jnp.int32, (ts, 1), 0) & (SUBLANES - 1)
    d = 1
    while d < SUBLANES:
        keep = row >= d
        b = jnp.where(keep, a * pltpu.roll(b, d, 0) + b, b)
        a = jnp.where(keep, a * pltpu.roll(a, d, 0), a)
        d *= 2
    a_sc[...] = a
    b_sc[...] = b

    def group(gidx, carry):
        rows = pl.ds(pl.multiple_of(gidx * SUBLANES, SUBLANES), SUBLANES)
        h8 = b_sc[rows, :] + a_sc[rows, :] * carry
        b_sc[rows, :] = h8
        return h8[SUBLANES - 1:SUBLANES, :]

    carry_sc[...] = lax.fori_loop(0, ts // SUBLANES, group, carry_sc[...], unroll=8)
    o_ref[...] = (b_sc[...] * jax.nn.gelu(y_ref[...], approximate=True)).astype(BF16)


def _lru(lrux, lruy, cw, cb, wr, br, wi, bi, lam, layer):
    s = lrux.shape[0]
    ts = TS_LRU
    rows = pl.BlockSpec((ts, LRU_WIDTH), lambda i: (i, 0))
    vec = _resident((1, LRU_WIDTH))
    return pl.pallas_call(
        _lru_kernel,
        grid=(s // ts,),
        in_specs=[rows, rows, _resident((CONV_W, LRU_WIDTH)), vec, _layer_resident(wr, layer), vec,
                  _layer_resident(wi, layer), vec, vec],
        out_specs=rows,
        out_shape=jax.ShapeDtypeStruct((s, LRU_WIDTH), BF16),
        scratch_shapes=[pltpu.VMEM((ts + SUBLANES, LRU_WIDTH), F32),
                        pltpu.VMEM((ts, LRU_WIDTH), F32),
                        pltpu.VMEM((ts, LRU_WIDTH), F32),
                        pltpu.VMEM((1, LRU_WIDTH), F32)],
        compiler_params=_params("arbitrary"),
        name="rglru",
    )(lrux, lruy, cw, cb, wr, br, wi, bi, lam)


NEG = -1e30
LOG2_E = math.log2(math.e)


def _pair_tables(n_blk):
    qi, kj = [], []
    for i in range(n_blk):
        for j in range(i + 1):
            qi.append(i)
            kj.append(j)
    return jnp.asarray(qi, jnp.int32), jnp.asarray(kj, jnp.int32)


def _online_softmax_step(st, vt1, m_sc, acc_sc, idx, qs):
    m_prev = m_sc[idx, :, qs]
    m_new = jnp.maximum(m_prev, jnp.max(st, axis=0, keepdims=True))
    alpha = jnp.exp2(m_prev - m_new)
    pt = jnp.exp2((st - m_new).astype(BF16))
    acc_sc[idx, :, qs] = alpha * acc_sc[idx, :, qs] + jnp.dot(vt1, pt, preferred_element_type=F32)
    m_sc[idx, :, qs] = m_new


def _block_parts(diagonal):
    full, half = slice(0, T_ATT), T_ATT // 2
    if not diagonal:
        return [(full, full)]
    return [(slice(0, half), full), (slice(half, T_ATT), slice(half, T_ATT))]


def _kq(k, q):
    return lax.dot_general(k, q, (((1,), (1,)), ((), ())), preferred_element_type=F32)


def _chunk_mask(t):
    shift = CHUNK.bit_length() - 1
    key = lax.broadcasted_iota(jnp.int32, (t, t), 0) >> shift
    qry = lax.broadcasted_iota(jnp.int32, (t, t), 1) >> shift
    return key <= qry


def _pipelined_maps(scores, values, m_sc, acc_sc, n_maps, diagonal):
    steps = [(idx, ks, qs) for ks, qs in _block_parts(diagonal) for idx in range(n_maps)]
    pending = [scores(*step) for step in steps[:QK_AHEAD]]
    for n, (idx, ks, qs) in enumerate(steps):
        if n + QK_AHEAD < len(steps):
            pending.append(scores(*steps[n + QK_AHEAD]))
        _online_softmax_step(pending.pop(0), values(idx, ks), m_sc, acc_sc, idx, qs)


def _init_stats(m_sc, acc_sc):
    m_sc[...] = jnp.full(m_sc.shape, NEG, F32)
    acc_sc[...] = jnp.zeros(acc_sc.shape, F32)


def _normalized(acc_sc, idx):
    return acc_sc[idx, 0:V_DIM, :] / acc_sc[idx, V_DIM:V_DIM + 1, :]


def _diff_attn_kernel(qi_ref, kj_ref, q_ref, k_ref, vt_ref, lq1_ref, lk1_ref, lq2_ref, lk2_ref,
                      gsub_ref, linit_ref, o_ref, m_sc, acc_sc):
    t = pl.program_id(0)
    i = qi_ref[t]
    j = kj_ref[t]

    @pl.when(j == 0)
    def _():
        _init_stats(m_sc, acc_sc)

    lane = lax.broadcasted_iota(jnp.int32, (1, LANES), 1)
    first_map = lane < DIFF_HEAD_DIM

    def block(diagonal):
        mask = _chunk_mask(T_ATT) if diagonal else None

        def scores(idx, ks, qs):
            h, m = divmod(idx, 2)
            cols = slice(h * LANES, (h + 1) * LANES)
            q = q_ref[qs, cols]
            zero = jnp.zeros_like(q)
            qm = jnp.where(first_map, q, zero) if m == 0 else jnp.where(first_map, zero, q)
            st = _kq(k_ref[ks, cols], qm)
            return jnp.where(mask[ks, qs], st, NEG) if diagonal else st

        _pipelined_maps(scores, lambda idx, ks: vt_ref[idx // 2, :, ks], m_sc, acc_sc,
                        2 * DIFF_HEADS, diagonal)

    @pl.when(j < i)
    def _():
        block(False)

    @pl.when(j == i)
    def _():
        block(True)
        linit = linit_ref[:, 0:1]
        lam = (jnp.exp(jnp.sum(lq1_ref[...] * lk1_ref[...], axis=1, keepdims=True))
               - jnp.exp(jnp.sum(lq2_ref[...] * lk2_ref[...], axis=1, keepdims=True)) + linit)
        for h in range(DIFF_HEADS):
            ot = _normalized(acc_sc, 2 * h) - lam * _normalized(acc_sc, 2 * h + 1)
            o = _rms(ot.T, gsub_ref[...], SUBLN_EPS) * (1.0 - linit)
            o_ref[:, h * LANES:(h + 1) * LANES] = o.astype(BF16)


def _diff_attention(dq, dk, dvt, lq1, lk1, lq2, lk2, gsub, linit, qi, kj):
    s = dq.shape[0]
    t = T_ATT
    qspec = pl.BlockSpec((t, DIFF_QK), lambda p, qi, kj: (qi[p], 0))
    kspec = pl.BlockSpec((t, DIFF_QK), lambda p, qi, kj: (kj[p], 0))
    vspec = pl.BlockSpec((DIFF_HEADS, V_ROWS, t), lambda p, qi, kj: (0, 0, kj[p]))
    small = lambda w: pl.BlockSpec((1, w), lambda p, qi, kj: (0, 0))
    nm = 2 * DIFF_HEADS
    return pl.pallas_call(
        _diff_attn_kernel,
        grid_spec=pltpu.PrefetchScalarGridSpec(
            num_scalar_prefetch=2,
            grid=(qi.shape[0],),
            in_specs=[qspec, kspec, vspec, small(DIFF_HEAD_DIM), small(DIFF_HEAD_DIM),
                      small(DIFF_HEAD_DIM), small(DIFF_HEAD_DIM), small(DIFF_V_DIM), small(LANES)],
            out_specs=qspec,
            scratch_shapes=[pltpu.VMEM((nm, 1, t), F32), pltpu.VMEM((nm, V_ROWS, t), F32)]),
        out_shape=jax.ShapeDtypeStruct((s, DIFF_OUT), BF16),
        compiler_params=_params("arbitrary"),
        name="diff_attention",
    )(qi, kj, dq, dk, dvt, lq1, lk1, lq2, lk2, gsub, linit)


def _mla_attn_kernel(qi_ref, kj_ref, q_ref, k_ref, vt_ref, o_ref, m_sc, acc_sc):
    t = pl.program_id(0)
    i = qi_ref[t]
    j = kj_ref[t]

    @pl.when(j == 0)
    def _():
        _init_stats(m_sc, acc_sc)

    def block(diagonal):
        mask = _chunk_mask(T_ATT) if diagonal else None

        def scores(h, ks, qs):
            st = _kq(k_ref[h, ks, :], q_ref[h, qs, :])
            return jnp.where(mask[ks, qs], st, NEG) if diagonal else st

        _pipelined_maps(scores, lambda h, ks: vt_ref[h, :, ks], m_sc, acc_sc, MLA_HEADS, diagonal)

    @pl.when(j < i)
    def _():
        block(False)

    @pl.when(j == i)
    def _():
        block(True)
        for h in range(MLA_HEADS):
            o_ref[:, h * LANES:(h + 1) * LANES] = _normalized(acc_sc, h).T.astype(BF16)


def _mla_attention(qm, km, vmt, qi, kj):
    s = qm.shape[1]
    t = T_ATT
    qspec = pl.BlockSpec((MLA_HEADS, t, MLA_QK), lambda p, qi, kj: (0, qi[p], 0))
    kspec = pl.BlockSpec((MLA_HEADS, t, MLA_QK), lambda p, qi, kj: (0, kj[p], 0))
    vspec = pl.BlockSpec((MLA_HEADS, V_ROWS, t), lambda p, qi, kj: (0, 0, kj[p]))
    ospec = pl.BlockSpec((t, MLA_OUT), lambda p, qi, kj: (qi[p], 0))
    return pl.pallas_call(
        _mla_attn_kernel,
        grid_spec=pltpu.PrefetchScalarGridSpec(
            num_scalar_prefetch=2,
            grid=(qi.shape[0],),
            in_specs=[qspec, kspec, vspec],
            out_specs=ospec,
            scratch_shapes=[pltpu.VMEM((MLA_HEADS, 1, t), F32), pltpu.VMEM((MLA_HEADS, V_ROWS, t), F32)]),
        out_shape=jax.ShapeDtypeStruct((s, MLA_OUT), BF16),
        compiler_params=_params("arbitrary"),
        name="mla_attention",
    )(qi, kj, qm, km, vmt)


def _outproj_kernel(x_ref, a_ref, b_ref, c_ref, w_ref, o_ref, wb_sc):
    @pl.when(pl.program_id(0) == 0)
    def _():
        wb_sc[...] = w_ref[...].astype(BF16)

    acc = jnp.dot(a_ref[...], wb_sc[0:LRU_WIDTH, :], preferred_element_type=F32)
    acc = acc + jnp.dot(b_ref[...], wb_sc[LRU_WIDTH:LRU_WIDTH + DIFF_OUT, :], preferred_element_type=F32)
    acc = acc + jnp.dot(c_ref[...], wb_sc[LRU_WIDTH + DIFF_OUT:D_MODEL, :], preferred_element_type=F32)
    o_ref[...] = x_ref[...] + acc


def _outproj(x, a, b, c, w, layer):
    s = x.shape[0]
    tm = TM_OUT
    rows = lambda wd: pl.BlockSpec((tm, wd), lambda i: (i, 0))
    return pl.pallas_call(
        _outproj_kernel,
        grid=(s // tm,),
        in_specs=[rows(D_MODEL), rows(LRU_WIDTH), rows(DIFF_OUT), rows(MLA_OUT),
                  _layer_resident(w, layer)],
        out_specs=rows(D_MODEL),
        out_shape=jax.ShapeDtypeStruct((s, D_MODEL), F32),
        scratch_shapes=[pltpu.VMEM((D_MODEL, D_MODEL), BF16)],
        compiler_params=_params("arbitrary"),
        name="outproj",
    )(x, a, b, c, w)


def _ffn_kernel(x_ref, g_ref, wg_ref, wu_ref, wd_ref, gf_ref, o_ref, hb_sc, *, final):
    f = pl.program_id(1)

    @pl.when(f == 0)
    def _():
        x = x_ref[...]
        hb_sc[...] = _rms(x, g_ref[...], NORM_EPS).astype(BF16)
        o_ref[...] = x

    hb = hb_sc[...]
    gate = jnp.dot(hb, wg_ref[...].astype(BF16), preferred_element_type=F32)
    up = jnp.dot(hb, wu_ref[...].astype(BF16), preferred_element_type=F32)
    act = (gate * jax.nn.sigmoid(gate) * up).astype(BF16)
    o_ref[...] += jnp.dot(act, wd_ref[...].astype(BF16), preferred_element_type=F32)

    if final:
        @pl.when(f == pl.num_programs(1) - 1)
        def _():
            o_ref[...] = _rms(o_ref[...], gf_ref[...], NORM_EPS)


def _ffn(x, g, wg, wu, wd, gf, layer, final):
    s = x.shape[0]
    tm, tf = TM_FFN, TF_FFN
    rows = pl.BlockSpec((tm, D_MODEL), lambda i, f: (i, 0))
    rows_in = pl.BlockSpec((tm, D_MODEL), lambda i, f: (i, 0), pipeline_mode=pl.Buffered(1))
    vec = pl.BlockSpec((1, D_MODEL), lambda i, f: (0, 0))
    return pl.pallas_call(
        functools.partial(_ffn_kernel, final=final),
        grid=(s // tm, D_FF // tf),
        in_specs=[rows_in, vec, pl.BlockSpec((None, D_MODEL, tf), lambda i, f: (layer, 0, f)),
                  pl.BlockSpec((None, D_MODEL, tf), lambda i, f: (layer, 0, f)),
                  pl.BlockSpec((None, tf, D_MODEL), lambda i, f: (layer, f, 0)), vec],
        out_specs=rows,
        out_shape=jax.ShapeDtypeStruct((s, D_MODEL), F32),
        scratch_shapes=[pltpu.VMEM((tm, D_MODEL), BF16)],
        compiler_params=_params("parallel", "arbitrary"),
        name="ffn",
    )(x, g, wg, wu, wd, gf)


def kernel(x, positions, g_mix, w_in, conv_w, conv_b, w_r, b_r, w_i, b_i, lru_lambda, lam_q1, lam_k1,
           lam_q2, lam_k2, g_sub, g_q_a, w_q_b, g_kv_a, w_kv_b, w_out, g_ffn, w_gate, w_up, w_down,
           g_final):
    batch, seq, _ = x.shape
    assert batch == 1 and seq % TM_FFN == 0 and seq % T_ATT == 0
    xs = x.reshape(seq, D_MODEL)
    pos_col = positions.reshape(seq, 1)

    dtab = _rope_tables(pos_col, DIFF_HEAD_DIM, DIFF_ROT)
    mtab = _rope_tables(pos_col, LANES, MLA_ROPE)
    qi, kj = _pair_tables(seq // T_ATT)

    win = jnp.pad(w_in.astype(BF16), ((0, 0), (0, 0), (0, IN_WIDTH_PAD - IN_WIDTH)))
    wqb = jnp.pad(w_q_b.reshape(DEPTH, MLA_Q_RANK, MLA_HEADS, MLA_QK),
                  ((0, 0), (0, 0), (0, 0), (0, 2 * LANES - MLA_QK)))
    wqb = wqb.reshape(DEPTH, MLA_Q_RANK, MLA_HEADS * 2 * LANES).astype(BF16)
    wkvb = w_kv_b.astype(BF16)
    wr = w_r.astype(BF16)
    wi = w_i.astype(BF16)
    row = lambda v: v.reshape(1, -1)

    for l in range(DEPTH):
        lambda_init = 0.8 - 0.6 * math.exp(-0.3 * l)
        linit = jnp.full((1, LANES), lambda_init, F32)
        lrux, lruy, dq, dk, dvt, qm, km, vmt = _inproj(
            xs, row(g_mix[l]), win, wqb, wkvb, row(g_q_a[l]), row(g_kv_a[l]), dtab, mtab, l)
        out_a = _lru(lrux, lruy, conv_w[l], row(conv_b[l]), wr, row(b_r[l]), wi, row(b_i[l]),
                     row(lru_lambda[l]), l)
        out_b = _diff_attention(dq, dk, dvt, row(lam_q1[l]), row(lam_k1[l]), row(lam_q2[l]),
                                row(lam_k2[l]), row(g_sub[l]), linit, qi, kj)
        out_c = _mla_attention(qm, km, vmt, qi, kj)
        xs = _outproj(xs, out_a, out_b, out_c, w_out, l)
        xs = _ffn(xs, row(g_ffn[l]), w_gate, w_up, w_down, row(g_final), l, final=(l == DEPTH - 1))
    return xs.reshape(batch, seq, D_MODEL)
```

```python
import functools
import math

import jax
import jax.numpy as jnp
import numpy as np
from jax import lax
from jax.experimental import pallas as pl
from jax.experimental.pallas import tpu as pltpu

D_MODEL = 2048
SEQ = 8192
DEPTH = 4
CHUNK = 64
ROPE_THETA = 500000.0
NORM_EPS = 1e-6

LRU_WIDTH = 768
LRU_BLOCKS = 6
LRU_BLOCK_W = 128
CONV_W = 4
LRU_C = 8.0

DIFF_HEADS = 4
DIFF_HEAD_DIM = 64
DIFF_V_DIM = 128
DIFF_QK = 512
DIFF_OUT = 512
DIFF_ROT = 16
SUBLN_EPS = 1e-5

MLA_HEADS = 6
MLA_NOPE = 128
MLA_ROPE = 64
MLA_V = 128
MLA_QK = MLA_NOPE + MLA_ROPE
MLA_Q_RANK = 512
MLA_KV_RANK = 256
MLA_OUT = 768

IN_WIDTH = 3904
D_FF = 5632

LANES = 128
SUBLANES = 8
VMEM_LIMIT = 60 * 1024 * 1024

TM_PROJ = 512
TS_LRU = 512
T_ATT = 1024
QK_AHEAD = 1
V_DIM = 128
BF16_ROWS = 16
V_ROWS = V_DIM + BF16_ROWS
TM_OUT = 512
TM_FFN = 1024
TF_FFN = 512

F32 = jnp.float32
BF16 = jnp.bfloat16


def _params(*sem):
    return pltpu.CompilerParams(dimension_semantics=sem, vmem_limit_bytes=VMEM_LIMIT)


def _resident(shape):
    nd = len(shape)
    return pl.BlockSpec(shape, lambda *_: (0,) * nd, pipeline_mode=pl.Buffered(1))


def _layer_resident(stacked, layer):
    nd = stacked.ndim - 1
    return pl.BlockSpec((None,) + stacked.shape[1:], lambda *_: (layer,) + (0,) * nd,
                        pipeline_mode=pl.Buffered(1))


def _rms(x, g, eps):
    return x * lax.rsqrt(jnp.mean(x * x, axis=-1, keepdims=True) + eps) * g


def _rope_table_kernel(pos_ref, invf_ref, lo_ref, hi_ref, c_ref, sp_ref, sm_ref):
    ang = pos_ref[...].astype(F32) * invf_ref[...]
    lo = lo_ref[...]
    hi = hi_ref[...]
    c_ref[...] = jnp.where(lo + hi > 0.0, jnp.cos(ang), 1.0)
    s = jnp.sin(ang)
    sp_ref[...] = s * hi
    sm_ref[...] = -s * lo


def _rope_tables(pos_col, group, rot_dim):
    half = rot_dim // 2
    inv_freq = ROPE_THETA ** (-jnp.arange(half, dtype=F32) / half)
    lane = np.arange(LANES) % group
    sel = jnp.asarray(lane % half)
    rot = lane < rot_dim
    invf = jnp.where(jnp.asarray(rot), inv_freq[sel], 0.0).reshape(1, LANES)
    lo = jnp.asarray((lane < half).astype(np.float32)).reshape(1, LANES)
    hi = jnp.asarray(((lane >= half) & rot).astype(np.float32)).reshape(1, LANES)
    s = pos_col.shape[0]
    tm = 1024
    row = pl.BlockSpec((1, LANES), lambda i: (0, 0))
    out = pl.BlockSpec((tm, LANES), lambda i: (i, 0))
    return pl.pallas_call(
        _rope_table_kernel,
        grid=(s // tm,),
        in_specs=[pl.BlockSpec((tm, 1), lambda i: (i, 0)), row, row, row],
        out_specs=[out, out, out],
        out_shape=[jax.ShapeDtypeStruct((s, LANES), F32)] * 3,
        compiler_params=_params("parallel"),
        name="rope_tables",
    )(pos_col, invf, lo, hi)


def _rope_cols(x, c, sp, sm, half):
    return x * c + pltpu.roll(x, half, 1) * sp + pltpu.roll(x, LANES - half, 1) * sm


def _rope_tail(x, c, sp, sm):
    x = pltpu.roll(x, MLA_ROPE, 1)
    return _rope_cols(x, c, sp, sm, MLA_ROPE // 2)[:, :MLA_ROPE]


def _inproj_kernel(x_ref, g_ref, win_ref, wqb_ref, wkvb_ref, gqa_ref, gkva_ref,
                   dc_ref, dsp_ref, dsm_ref, mc_ref, msp_ref, msm_ref,
                   lrux_ref, lruy_ref, dq_ref, dk_ref, dvt_ref, qm_ref, km_ref, vmt_ref):
    hb = _rms(x_ref[...], g_ref[...], NORM_EPS).astype(BF16)

    def proj(lo, hi):
        return jnp.dot(hb, win_ref[:, lo:hi], preferred_element_type=F32)

    mc, msp, msm = mc_ref[...], msp_ref[...], msm_ref[...]
    qa = proj(3072, 3584)
    qn = _rms(qa, gqa_ref[...], NORM_EPS).astype(BF16)
    kva = proj(3584, IN_WIDTH)
    kr = _rope_tail(kva[:, IN_WIDTH - 3584 - LANES:], mc, msp, msm).astype(BF16)
    kvn = _rms(kva[:, :MLA_KV_RANK], gkva_ref[...], NORM_EPS).astype(BF16)

    dc, dsp, dsm = dc_ref[...], dsp_ref[...], dsm_ref[...]
    diff_scale = DIFF_HEAD_DIM ** -0.5 * LOG2_E
    dq = proj(1536, 2048)
    for h in range(DIFF_HEADS):
        cols = slice(h * LANES, (h + 1) * LANES)
        dq_ref[:, cols] = (_rope_cols(dq[:, cols], dc, dsp, dsm, DIFF_ROT // 2) * diff_scale).astype(BF16)

    q = jnp.dot(qn, wqb_ref[...], preferred_element_type=F32)
    mla_scale = MLA_QK ** -0.5 * LOG2_E
    for h in range(MLA_HEADS):
        base = h * MLA_QK
        qm_ref[h, :, 0:MLA_NOPE] = (q[:, base:base + MLA_NOPE] * mla_scale).astype(BF16)
        qr = _rope_tail(q[:, base + MLA_QK - LANES:base + MLA_QK], mc, msp, msm)
        qm_ref[h, :, MLA_NOPE:MLA_QK] = (qr * mla_scale).astype(BF16)

    dk = proj(2048, 2560)
    for h in range(DIFF_HEADS):
        cols = slice(h * LANES, (h + 1) * LANES)
        dk_ref[:, cols] = _rope_cols(dk[:, cols], dc, dsp, dsm, DIFF_ROT // 2).astype(BF16)

    ones = jnp.ones((BF16_ROWS, x_ref.shape[0]), BF16)
    kv = jnp.dot(kvn, wkvb_ref[...], preferred_element_type=F32)
    for h in range(MLA_HEADS):
        base = h * 2 * LANES
        km_ref[h, :, 0:MLA_NOPE] = kv[:, base:base + LANES].astype(BF16)
        km_ref[h, :, MLA_NOPE:MLA_QK] = kr
        vmt_ref[h, 0:V_DIM, :] = kv[:, base + LANES:base + 2 * LANES].T.astype(BF16)
        vmt_ref[h, V_DIM:V_ROWS, :] = ones

    dv = proj(2560, 3072)
    for h in range(DIFF_HEADS):
        dvt_ref[h, 0:V_DIM, :] = dv[:, h * LANES:(h + 1) * LANES].T.astype(BF16)
        dvt_ref[h, V_DIM:V_ROWS, :] = ones

    lrux_ref[...] = proj(0, 768)
    lruy_ref[...] = proj(768, 1536)


def _inproj(x, g, win, wqb, wkvb, gqa, gkva, dtab, mtab, layer):
    s = x.shape[0]
    tm = TM_PROJ
    rows = lambda w: pl.BlockSpec((tm, w), lambda i: (i, 0))
    heads = pl.BlockSpec((MLA_HEADS, tm, MLA_QK), lambda i: (0, i, 0))
    vt = lambda nh: pl.BlockSpec((nh, V_ROWS, tm), lambda i: (0, 0, i))
    tab = rows(LANES)
    return pl.pallas_call(
        _inproj_kernel,
        grid=(s // tm,),
        in_specs=[rows(D_MODEL), _resident((1, D_MODEL)), _layer_resident(win, layer),
                  _layer_resident(wqb, layer), _layer_resident(wkvb, layer),
                  _resident((1, MLA_Q_RANK)), _resident((1, MLA_KV_RANK)),
                  tab, tab, tab, tab, tab, tab],
        out_specs=[rows(LRU_WIDTH), rows(LRU_WIDTH), rows(DIFF_QK), rows(DIFF_QK), vt(DIFF_HEADS),
                   heads, heads, vt(MLA_HEADS)],
        out_shape=[jax.ShapeDtypeStruct((s, LRU_WIDTH), F32), jax.ShapeDtypeStruct((s, LRU_WIDTH), F32),
                   jax.ShapeDtypeStruct((s, DIFF_QK), BF16), jax.ShapeDtypeStruct((s, DIFF_QK), BF16),
                   jax.ShapeDtypeStruct((DIFF_HEADS, V_ROWS, s), BF16),
                   jax.ShapeDtypeStruct((MLA_HEADS, s, MLA_QK), BF16),
                   jax.ShapeDtypeStruct((MLA_HEADS, s, MLA_QK), BF16),
                   jax.ShapeDtypeStruct((MLA_HEADS, V_ROWS, s), BF16)],
        compiler_params=_params("parallel"),
        name="inproj",
    )(x, g, win, wqb, wkvb, gqa, gkva, *dtab, *mtab)


def _lru_kernel(x_ref, y_ref, cw_ref, cb_ref, wr_ref, br_ref, wi_ref, bi_ref, lam_ref, o_ref,
                xext_sc, a_sc, b_sc, carry_sc):
    ts = x_ref.shape[0]
    pad = SUBLANES

    @pl.when(pl.program_id(0) == 0)
    def _():
        xext_sc[0:pad, :] = jnp.zeros((pad, LRU_WIDTH), F32)
        carry_sc[...] = jnp.zeros_like(carry_sc)

    xext_sc[pad:pad + ts, :] = x_ref[...]
    xc = cb_ref[...] + cw_ref[CONV_W - 1:CONV_W, :] * xext_sc[pad:pad + ts, :]
    for k in range(1, CONV_W):
        xc = xc + cw_ref[CONV_W - 1 - k:CONV_W - k, :] * xext_sc[pad - k:pad - k + ts, :]
    xext_sc[0:pad, :] = x_ref[ts - pad:ts, :]

    xcb = xc.astype(BF16)
    r_parts, i_parts = [], []
    for h in range(LRU_BLOCKS):
        cols = slice(h * LRU_BLOCK_W, (h + 1) * LRU_BLOCK_W)
        r_parts.append(jnp.dot(xcb[:, cols], wr_ref[h], preferred_element_type=F32))
        i_parts.append(jnp.dot(xcb[:, cols], wi_ref[h], preferred_element_type=F32))
    r = jax.nn.sigmoid(jnp.concatenate(r_parts, axis=1) + br_ref[...])
    ig = jax.nn.sigmoid(jnp.concatenate(i_parts, axis=1) + bi_ref[...])

    z = -lam_ref[...]
    softplus = jnp.maximum(z, 0.0) + jnp.log1p(jnp.exp(-jnp.abs(z)))
    log_a = (-LRU_C) * r * softplus
    a = jnp.exp(log_a)
    t = 1.0 - a * a
    b = jnp.where(t > 0.0, t * lax.rsqrt(t), 0.0) * (ig * xc)

    row = lax.broadcasted_iota(jnp.int32, (ts, 1), 0) & (SUBLANES - 1)
    d = 1
    while d < SUBLANES:
        keep = row >= d
        b = jnp.where(keep, a * pltpu.roll(b, d, 0) + b, b)
        a = jnp.where(keep, a * pltpu.roll(a, d, 0), a)
        d *= 2
    a_sc[...] = a
    b_sc[...] = b

    def group(gidx, carry):
        rows = pl.ds(pl.multiple_of(gidx * SUBLANES, SUBLANES), SUBLANES)
        h8 = b_sc[rows, :] + a_sc[rows, :] * carry
        b_sc[rows, :] = h8
        return h8[SUBLANES - 1:SUBLANES, :]

    carry_sc[...] = lax.fori_loop(0, ts // SUBLANES, group, carry_sc[...], unroll=8)
    o_ref[...] = (b_sc[...] * jax.nn.gelu(y_ref[...], approximate=True)).astype(BF16)


def _lru(lrux, lruy, cw, cb, wr, br, wi, bi, lam, layer):
    s = lrux.shape[0]
    ts = TS_LRU
    rows = pl.BlockSpec((ts, LRU_WIDTH), lambda i: (i, 0))
    vec = _resident((1, LRU_WIDTH))
    return pl.pallas_call(
        _lru_kernel,
        grid=(s // ts,),
        in_specs=[rows, rows, _resident((CONV_W, LRU_WIDTH)), vec, _layer_resident(wr, layer), vec,
                  _layer_resident(wi, layer), vec, vec],
        out_specs=rows,
        out_shape=jax.ShapeDtypeStruct((s, LRU_WIDTH), BF16),
        scratch_shapes=[pltpu.VMEM((ts + SUBLANES, LRU_WIDTH), F32),
                        pltpu.VMEM((ts, LRU_WIDTH), F32),
                        pltpu.VMEM((ts, LRU_WIDTH), F32),
                        pltpu.VMEM((1, LRU_WIDTH), F32)],
        compiler_params=_params("arbitrary"),
        name="rglru",
    )(lrux, lruy, cw, cb, wr, br, wi, bi, lam)


NEG = -1e30
LOG2_E = math.log2(math.e)


def _pair_tables(n_blk):
    qi, kj = [], []
    for i in range(n_blk):
        for j in range(i + 1):
            qi.append(i)
            kj.append(j)
    return jnp.asarray(qi, jnp.int32), jnp.asarray(kj, jnp.int32)


def _online_softmax_step(st, vt1, m_sc, acc_sc, idx, qs):
    m_prev = m_sc[idx, :, qs]
    m_new = jnp.maximum(m_prev, jnp.max(st, axis=0, keepdims=True))
    alpha = jnp.exp2(m_prev - m_new)
    pt = jnp.exp2((st - m_new).astype(BF16))
    acc_sc[idx, :, qs] = alpha * acc_sc[idx, :, qs] + jnp.dot(vt1, pt, preferred_element_type=F32)
    m_sc[idx, :, qs] = m_new


def _block_parts(diagonal):
    full, half = slice(0, T_ATT), T_ATT // 2
    if not diagonal:
        return [(full, full)]
    return [(slice(0, half), full), (slice(half, T_ATT), slice(half, T_ATT))]


def _kq(k, q):
    return lax.dot_general(k, q, (((1,), (1,)), ((), ())), preferred_element_type=F32)


def _chunk_mask(t):
    shift = CHUNK.bit_length() - 1
    key = lax.broadcasted_iota(jnp.int32, (t, t), 0) >> shift
    qry = lax.broadcasted_iota(jnp.int32, (t, t), 1) >> shift
    return key <= qry


def _pipelined_maps(scores, values, m_sc, acc_sc, n_maps, diagonal):
    steps = [(idx, ks, qs) for ks, qs in _block_parts(diagonal) for idx in range(n_maps)]
    pending = [scores(*step) for step in steps[:QK_AHEAD]]
    for n, (idx, ks, qs) in enumerate(steps):
        if n + QK_AHEAD < len(steps):
            pending.append(scores(*steps[n + QK_AHEAD]))
        _online_softmax_step(pending.pop(0), values(idx, ks), m_sc, acc_sc, idx, qs)


def _init_stats(m_sc, acc_sc):
    m_sc[...] = jnp.full(m_sc.shape, NEG, F32)
    acc_sc[...] = jnp.zeros(acc_sc.shape, F32)


def _normalized(acc_sc, idx):
    return acc_sc[idx, 0:V_DIM, :] / acc_sc[idx, V_DIM:V_DIM + 1, :]


def _diff_attn_kernel(qi_ref, kj_ref, q_ref, k_ref, vt_ref, lq1_ref, lk1_ref, lq2_ref, lk2_ref,
                      gsub_ref, linit_ref, o_ref, m_sc, acc_sc):
    t = pl.program_id(0)
    i = qi_ref[t]
    j = kj_ref[t]

    @pl.when(j == 0)
    def _():
        _init_stats(m_sc, acc_sc)

    lane = lax.broadcasted_iota(jnp.int32, (1, LANES), 1)
    first_map = lane < DIFF_HEAD_DIM

    def block(diagonal):
        mask = _chunk_mask(T_ATT) if diagonal else None

        def scores(idx, ks, qs):
            h, m = divmod(idx, 2)
            cols = slice(h * LANES, (h + 1) * LANES)
            q = q_ref[qs, cols]
            zero = jnp.zeros_like(q)
            qm = jnp.where(first_map, q, zero) if m == 0 else jnp.where(first_map, zero, q)
            st = _kq(k_ref[ks, cols], qm)
            return jnp.where(mask[ks, qs], st, NEG) if diagonal else st

        _pipelined_maps(scores, lambda idx, ks: vt_ref[idx // 2, :, ks], m_sc, acc_sc,
                        2 * DIFF_HEADS, diagonal)

    @pl.when(j < i)
    def _():
        block(False)

    @pl.when(j == i)
    def _():
        block(True)
        linit = linit_ref[:, 0:1]
        lam = (jnp.exp(jnp.sum(lq1_ref[...] * lk1_ref[...], axis=1, keepdims=True))
               - jnp.exp(jnp.sum(lq2_ref[...] * lk2_ref[...], axis=1, keepdims=True)) + linit)
        for h in range(DIFF_HEADS):
            ot = _normalized(acc_sc, 2 * h) - lam * _normalized(acc_sc, 2 * h + 1)
            o = _rms(ot.T, gsub_ref[...], SUBLN_EPS) * (1.0 - linit)
            o_ref[:, h * LANES:(h + 1) * LANES] = o.astype(BF16)


def _diff_attention(dq, dk, dvt, lq1, lk1, lq2, lk2, gsub, linit, qi, kj):
    s = dq.shape[0]
    t = T_ATT
    qspec = pl.BlockSpec((t, DIFF_QK), lambda p, qi, kj: (qi[p], 0))
    kspec = pl.BlockSpec((t, DIFF_QK), lambda p, qi, kj: (kj[p], 0))
    vspec = pl.BlockSpec((DIFF_HEADS, V_ROWS, t), lambda p, qi, kj: (0, 0, kj[p]))
    small = lambda w: pl.BlockSpec((1, w), lambda p, qi, kj: (0, 0))
    nm = 2 * DIFF_HEADS
    return pl.pallas_call(
        _diff_attn_kernel,
        grid_spec=pltpu.PrefetchScalarGridSpec(
            num_scalar_prefetch=2,
            grid=(qi.shape[0],),
            in_specs=[qspec, kspec, vspec, small(DIFF_HEAD_DIM), small(DIFF_HEAD_DIM),
                      small(DIFF_HEAD_DIM), small(DIFF_HEAD_DIM), small(DIFF_V_DIM), small(LANES)],
            out_specs=qspec,
            scratch_shapes=[pltpu.VMEM((nm, 1, t), F32), pltpu.VMEM((nm, V_ROWS, t), F32)]),
        out_shape=jax.ShapeDtypeStruct((s, DIFF_OUT), BF16),
        compiler_params=_params("arbitrary"),
        name="diff_attention",
    )(qi, kj, dq, dk, dvt, lq1, lk1, lq2, lk2, gsub, linit)


def _mla_attn_kernel(qi_ref, kj_ref, q_ref, k_ref, vt_ref, o_ref, m_sc, acc_sc):
    t = pl.program_id(0)
    i = qi_ref[t]
    j = kj_ref[t]

    @pl.when(j == 0)
    def _():
        _init_stats(m_sc, acc_sc)

    def block(diagonal):
        mask = _chunk_mask(T_ATT) if diagonal else None

        def scores(h, ks, qs):
            st = _kq(k_ref[h, ks, :], q_ref[h, qs, :])
            return jnp.where(mask[ks, qs], st, NEG) if diagonal else st

        _pipelined_maps(scores, lambda h, ks: vt_ref[h, :, ks], m_sc, acc_sc, MLA_HEADS, diagonal)

    @pl.when(j < i)
    def _():
        block(False)

    @pl.when(j == i)
    def _():
        block(True)
        for h in range(MLA_HEADS):
            o_ref[:, h * LANES:(h + 1) * LANES] = _normalized(acc_sc, h).T.astype(BF16)


def _mla_attention(qm, km, vmt, qi, kj):
    s = qm.shape[1]
    t = T_ATT
    qspec = pl.BlockSpec((MLA_HEADS, t, MLA_QK), lambda p, qi, kj: (0, qi[p], 0))
    kspec = pl.BlockSpec((MLA_HEADS, t, MLA_QK), lambda p, qi, kj: (0, kj[p], 0))
    vspec = pl.BlockSpec((MLA_HEADS, V_ROWS, t), lambda p, qi, kj: (0, 0, kj[p]))
    ospec = pl.BlockSpec((t, MLA_OUT), lambda p, qi, kj: (qi[p], 0))
    return pl.pallas_call(
        _mla_attn_kernel,
        grid_spec=pltpu.PrefetchScalarGridSpec(
            num_scalar_prefetch=2,
            grid=(qi.shape[0],),
            in_specs=[qspec, kspec, vspec],
            out_specs=ospec,
            scratch_shapes=[pltpu.VMEM((MLA_HEADS, 1, t), F32), pltpu.VMEM((MLA_HEADS, V_ROWS, t), F32)]),
        out_shape=jax.ShapeDtypeStruct((s, MLA_OUT), BF16),
        compiler_params=_params("arbitrary"),
        name="mla_attention",
    )(qi, kj, qm, km, vmt)


def _outproj_kernel(x_ref, a_ref, b_ref, c_ref, w_ref, o_ref, wb_sc):
    @pl.when(pl.program_id(0) == 0)
    def _():
        wb_sc[...] = w_ref[...].astype(BF16)

    acc = jnp.dot(a_ref[...], wb_sc[0:LRU_WIDTH, :], preferred_element_type=F32)
    acc = acc + jnp.dot(b_ref[...], wb_sc[LRU_WIDTH:LRU_WIDTH + DIFF_OUT, :], preferred_element_type=F32)
    acc = acc + jnp.dot(c_ref[...], wb_sc[LRU_WIDTH + DIFF_OUT:D_MODEL, :], preferred_element_type=F32)
    o_ref[...] = x_ref[...] + acc


def _outproj(x, a, b, c, w, layer):
    s = x.shape[0]
    tm = TM_OUT
    rows = lambda wd: pl.BlockSpec((tm, wd), lambda i: (i, 0))
    return pl.pallas_call(
        _outproj_kernel,
        grid=(s // tm,),
        in_specs=[rows(D_MODEL), rows(LRU_WIDTH), rows(DIFF_OUT), rows(MLA_OUT),
                  _layer_resident(w, layer)],
        out_specs=rows(D_MODEL),
        out_shape=jax.ShapeDtypeStruct((s, D_MODEL), F32),
        scratch_shapes=[pltpu.VMEM((D_MODEL, D_MODEL), BF16)],
        compiler_params=_params("arbitrary"),
        name="outproj",
    )(x, a, b, c, w)


def _ffn_kernel(x_ref, g_ref, wg_ref, wu_ref, wd_ref, gf_ref, o_ref, hb_sc, *, final):
    f = pl.program_id(1)

    @pl.when(f == 0)
    def _():
        x = x_ref[...]
        hb_sc[...] = _rms(x, g_ref[...], NORM_EPS).astype(BF16)
        o_ref[...] = x

    hb = hb_sc[...]
    gate = jnp.dot(hb, wg_ref[...].astype(BF16), preferred_element_type=F32)
    up = jnp.dot(hb, wu_ref[...].astype(BF16), preferred_element_type=F32)
    act = (gate * jax.nn.sigmoid(gate) * up).astype(BF16)
    o_ref[...] += jnp.dot(act, wd_ref[...].astype(BF16), preferred_element_type=F32)

    if final:
        @pl.when(f == pl.num_programs(1) - 1)
        def _():
            o_ref[...] = _rms(o_ref[...], gf_ref[...], NORM_EPS)


def _ffn(x, g, wg, wu, wd, gf, layer, final):
    s = x.shape[0]
    tm, tf = TM_FFN, TF_FFN
    rows = pl.BlockSpec((tm, D_MODEL), lambda i, f: (i, 0))
    rows_in = pl.BlockSpec((tm, D_MODEL), lambda i, f: (i, 0), pipeline_mode=pl.Buffered(1))
    vec = pl.BlockSpec((1, D_MODEL), lambda i, f: (0, 0))
    return pl.pallas_call(
        functools.partial(_ffn_kernel, final=final),
        grid=(s // tm, D_FF // tf),
        in_specs=[rows_in, vec, pl.BlockSpec((None, D_MODEL, tf), lambda i, f: (layer, 0, f)),
                  pl.BlockSpec((None, D_MODEL, tf), lambda i, f: (layer, 0, f)),
                  pl.BlockSpec((None, tf, D_MODEL), lambda i, f: (layer, f, 0)), vec],
        out_specs=rows,
        out_shape=jax.ShapeDtypeStruct((s, D_MODEL), F32),
        scratch_shapes=[pltpu.VMEM((tm, D_MODEL), BF16)],
        compiler_params=_params("parallel", "arbitrary"),
        name="ffn",
    )(x, g, wg, wu, wd, gf)


def kernel(x, positions, g_mix, w_in, conv_w, conv_b, w_r, b_r, w_i, b_i, lru_lambda, lam_q1, lam_k1,
           lam_q2, lam_k2, g_sub, g_q_a, w_q_b, g_kv_a, w_kv_b, w_out, g_ffn, w_gate, w_up, w_down,
           g_final):
    batch, seq, _ = x.shape
    assert batch == 1 and seq % TM_FFN == 0 and seq % T_ATT == 0
    xs = x.reshape(seq, D_MODEL)
    pos_col = positions.reshape(seq, 1)

    dtab = _rope_tables(pos_col, DIFF_HEAD_DIM, DIFF_ROT)
    mtab = _rope_tables(pos_col, LANES, MLA_ROPE)
    qi, kj = _pair_tables(seq // T_ATT)

    win = w_in.astype(BF16)
    wqb = w_q_b.astype(BF16)
    wkvb = w_kv_b.astype(BF16)
    wr = w_r.astype(BF16)
    wi = w_i.astype(BF16)
    row = lambda v: v.reshape(1, -1)

    for l in range(DEPTH):
        lambda_init = 0.8 - 0.6 * math.exp(-0.3 * l)
        linit = jnp.full((1, LANES), lambda_init, F32)
        lrux, lruy, dq, dk, dvt, qm, km, vmt = _inproj(
            xs, row(g_mix[l]), win, wqb, wkvb, row(g_q_a[l]), row(g_kv_a[l]), dtab, mtab, l)
        out_a = _lru(lrux, lruy, conv_w[l], row(conv_b[l]), wr, row(b_r[l]), wi, row(b_i[l]),
                     row(lru_lambda[l]), l)
        out_b = _diff_attention(dq, dk, dvt, row(lam_q1[l]), row(lam_k1[l]), row(lam_q2[l]),
                                row(lam_k2[l]), row(g_sub[l]), linit, qi, kj)
        out_c = _mla_attention(qm, km, vmt, qi, kj)
        xs = _outproj(xs, out_a, out_b, out_c, w_out, l)
        xs = _ffn(xs, row(g_ffn[l]), w_gate, w_up, w_down, row(g_final), l, final=(l == DEPTH - 1))
    return xs.reshape(batch, seq, D_MODEL)
```

```python
import functools
import math

import jax
import jax.numpy as jnp
import numpy as np
from jax import lax
from jax.experimental import pallas as pl
from jax.experimental.pallas import tpu as pltpu

D_MODEL = 2048
SEQ = 8192
DEPTH = 4
CHUNK = 64
ROPE_THETA = 500000.0
NORM_EPS = 1e-6

LRU_WIDTH = 768
LRU_BLOCKS = 6
LRU_BLOCK_W = 128
CONV_W = 4
LRU_C = 8.0

DIFF_HEADS = 4
DIFF_HEAD_DIM = 64
DIFF_V_DIM = 128
DIFF_QK = 512
DIFF_OUT = 512
DIFF_ROT = 16
SUBLN_EPS = 1e-5

MLA_HEADS = 6
MLA_NOPE = 128
MLA_ROPE = 64
MLA_V = 128
MLA_QK = MLA_NOPE + MLA_ROPE
MLA_Q_RANK = 512
MLA_KV_RANK = 256
MLA_OUT = 768

IN_WIDTH = 3904
D_FF = 5632

LANES = 128
SUBLANES = 8
VMEM_LIMIT = 60 * 1024 * 1024

TM_PROJ = 512
TS_LRU = 512
T_ATT = 1024
QK_AHEAD = 1
V_DIM = 128
BF16_ROWS = 16
V_ROWS = V_DIM + BF16_ROWS
TM_OUT = 512
TM_FFN = 1024
TF_FFN = 512

F32 = jnp.float32
BF16 = jnp.bfloat16


def _params(*sem):
    return pltpu.CompilerParams(dimension_semantics=sem, vmem_limit_bytes=VMEM_LIMIT)


def _resident(shape):
    nd = len(shape)
    return pl.BlockSpec(shape, lambda *_: (0,) * nd, pipeline_mode=pl.Buffered(1))


def _layer_resident(stacked, layer):
    nd = stacked.ndim - 1
    return pl.BlockSpec((None,) + stacked.shape[1:], lambda *_: (layer,) + (0,) * nd,
                        pipeline_mode=pl.Buffered(1))


def _rms(x, g, eps):
    return x * lax.rsqrt(jnp.mean(x * x, axis=-1, keepdims=True) + eps) * g


def _rope_table_kernel(pos_ref, invf_ref, lo_ref, hi_ref, c_ref, sp_ref, sm_ref):
    ang = pos_ref[...].astype(F32) * invf_ref[...]
    lo = lo_ref[...]
    hi = hi_ref[...]
    c_ref[...] = jnp.where(lo + hi > 0.0, jnp.cos(ang), 1.0)
    s = jnp.sin(ang)
    sp_ref[...] = s * hi
    sm_ref[...] = -s * lo


def _rope_tables(pos_col, group, rot_dim):
    half = rot_dim // 2
    inv_freq = ROPE_THETA ** (-jnp.arange(half, dtype=F32) / half)
    lane = np.arange(LANES) % group
    sel = jnp.asarray(lane % half)
    rot = lane < rot_dim
    invf = jnp.where(jnp.asarray(rot), inv_freq[sel], 0.0).reshape(1, LANES)
    lo = jnp.asarray((lane < half).astype(np.float32)).reshape(1, LANES)
    hi = jnp.asarray(((lane >= half) & rot).astype(np.float32)).reshape(1, LANES)
    s = pos_col.shape[0]
    tm = 1024
    row = pl.BlockSpec((1, LANES), lambda i: (0, 0))
    out = pl.BlockSpec((tm, LANES), lambda i: (i, 0))
    return pl.pallas_call(
        _rope_table_kernel,
        grid=(s // tm,),
        in_specs=[pl.BlockSpec((tm, 1), lambda i: (i, 0)), row, row, row],
        out_specs=[out, out, out],
        out_shape=[jax.ShapeDtypeStruct((s, LANES), F32)] * 3,
        compiler_params=_params("parallel"),
        name="rope_tables",
    )(pos_col, invf, lo, hi)


def _rope_cols(x, c, sp, sm, half):
    return x * c + pltpu.roll(x, half, 1) * sp + pltpu.roll(x, LANES - half, 1) * sm


def _rope_tail(x, c, sp, sm):
    x = pltpu.roll(x, MLA_ROPE, 1)
    return _rope_cols(x, c, sp, sm, MLA_ROPE // 2)[:, :MLA_ROPE]


def _inproj_kernel(x_ref, g_ref, win_ref, wqb_ref, wkvb_ref, gqa_ref, gkva_ref,
                   dc_ref, dsp_ref, dsm_ref, mc_ref, msp_ref, msm_ref,
                   lrux_ref, lruy_ref, dq_ref, dk_ref, dvt_ref, qm_ref, km_ref, vmt_ref):
    hb = _rms(x_ref[...], g_ref[...], NORM_EPS).astype(BF16)

    def proj(lo, hi):
        return jnp.dot(hb, win_ref[:, lo:hi], preferred_element_type=F32)

    mc, msp, msm = mc_ref[...], msp_ref[...], msm_ref[...]
    qa = proj(3072, 3584)
    qn = _rms(qa, gqa_ref[...], NORM_EPS).astype(BF16)
    kva = proj(3584, IN_WIDTH)
    kr = _rope_tail(kva[:, IN_WIDTH - 3584 - LANES:], mc, msp, msm).astype(BF16)
    kvn = _rms(kva[:, :MLA_KV_RANK], gkva_ref[...], NORM_EPS).astype(BF16)

    dc, dsp, dsm = dc_ref[...], dsp_ref[...], dsm_ref[...]
    diff_scale = DIFF_HEAD_DIM ** -0.5 * LOG2_E
    dq = proj(1536, 2048)
    for h in range(DIFF_HEADS):
        cols = slice(h * LANES, (h + 1) * LANES)
        dq_ref[:, cols] = (_rope_cols(dq[:, cols], dc, dsp, dsm, DIFF_ROT // 2) * diff_scale).astype(BF16)

    q = jnp.dot(qn, wqb_ref[...], preferred_element_type=F32)
    mla_scale = MLA_QK ** -0.5 * LOG2_E
    for h in range(MLA_HEADS):
        base = h * MLA_QK
        qm_ref[h, :, 0:MLA_NOPE] = (q[:, base:base + MLA_NOPE] * mla_scale).astype(BF16)
        qr = _rope_tail(q[:, base + MLA_QK - LANES:base + MLA_QK], mc, msp, msm)
        qm_ref[h, :, MLA_NOPE:MLA_QK] = (qr * mla_scale).astype(BF16)

    dk = proj(2048, 2560)
    for h in range(DIFF_HEADS):
        cols = slice(h * LANES, (h + 1) * LANES)
        dk_ref[:, cols] = _rope_cols(dk[:, cols], dc, dsp, dsm, DIFF_ROT // 2).astype(BF16)

    ones = jnp.ones((BF16_ROWS, x_ref.shape[0]), BF16)
    kv = jnp.dot(kvn, wkvb_ref[...], preferred_element_type=F32)
    for h in range(MLA_HEADS):
        base = h * 2 * LANES
        km_ref[h, :, 0:MLA_NOPE] = kv[:, base:base + LANES].astype(BF16)
        km_ref[h, :, MLA_NOPE:MLA_QK] = kr
        vmt_ref[h, 0:V_DIM, :] = kv[:, base + LANES:base + 2 * LANES].T.astype(BF16)
        vmt_ref[h, V_DIM:V_ROWS, :] = ones

    dv = proj(2560, 3072)
    for h in range(DIFF_HEADS):
        dvt_ref[h, 0:V_DIM, :] = dv[:, h * LANES:(h + 1) * LANES].T.astype(BF16)
        dvt_ref[h, V_DIM:V_ROWS, :] = ones

    lrux_ref[...] = proj(0, 768)
    lruy_ref[...] = proj(768, 1536)


def _inproj(x, g, win, wqb, wkvb, gqa, gkva, dtab, mtab, layer):
    s = x.shape[0]
    tm = TM_PROJ
    rows = lambda w: pl.BlockSpec((tm, w), lambda i: (i, 0))
    heads = pl.BlockSpec((MLA_HEADS, tm, MLA_QK), lambda i: (0, i, 0))
    vt = lambda nh: pl.BlockSpec((nh, V_ROWS, tm), lambda i: (0, 0, i))
    tab = rows(LANES)
    return pl.pallas_call(
        _inproj_kernel,
        grid=(s // tm,),
        in_specs=[rows(D_MODEL), _resident((1, D_MODEL)), _layer_resident(win, layer),
                  _layer_resident(wqb, layer), _layer_resident(wkvb, layer),
                  _resident((1, MLA_Q_RANK)), _resident((1, MLA_KV_RANK)),
                  tab, tab, tab, tab, tab, tab],
        out_specs=[rows(LRU_WIDTH), rows(LRU_WIDTH), rows(DIFF_QK), rows(DIFF_QK), vt(DIFF_HEADS),
                   heads, heads, vt(MLA_HEADS)],
        out_shape=[jax.ShapeDtypeStruct((s, LRU_WIDTH), F32), jax.ShapeDtypeStruct((s, LRU_WIDTH), F32),
                   jax.ShapeDtypeStruct((s, DIFF_QK), BF16), jax.ShapeDtypeStruct((s, DIFF_QK), BF16),
                   jax.ShapeDtypeStruct((DIFF_HEADS, V_ROWS, s), BF16),
                   jax.ShapeDtypeStruct((MLA_HEADS, s, MLA_QK), BF16),
                   jax.ShapeDtypeStruct((MLA_HEADS, s, MLA_QK), BF16),
                   jax.ShapeDtypeStruct((MLA_HEADS, V_ROWS, s), BF16)],
        compiler_params=_params("parallel"),
        name="inproj",
    )(x, g, win, wqb, wkvb, gqa, gkva, *dtab, *mtab)


def _lru_kernel(x_ref, y_ref, cw_ref, cb_ref, wr_ref, br_ref, wi_ref, bi_ref, lam_ref, o_ref,
                xext_sc, a_sc, b_sc, carry_sc):
    ts = x_ref.shape[0]
    pad = SUBLANES

    @pl.when(pl.program_id(0) == 0)
    def _():
        xext_sc[0:pad, :] = jnp.zeros((pad, LRU_WIDTH), F32)
        carry_sc[...] = jnp.zeros_like(carry_sc)

    xext_sc[pad:pad + ts, :] = x_ref[...]
    xc = cb_ref[...] + cw_ref[CONV_W - 1:CONV_W, :] * xext_sc[pad:pad + ts, :]
    for k in range(1, CONV_W):
        xc = xc + cw_ref[CONV_W - 1 - k:CONV_W - k, :] * xext_sc[pad - k:pad - k + ts, :]
    xext_sc[0:pad, :] = x_ref[ts - pad:ts, :]

    xcb = xc.astype(BF16)
    r_parts, i_parts = [], []
    for h in range(LRU_BLOCKS):
        cols = slice(h * LRU_BLOCK_W, (h + 1) * LRU_BLOCK_W)
        r_parts.append(jnp.dot(xcb[:, cols], wr_ref[h], preferred_element_type=F32))
        i_parts.append(jnp.dot(xcb[:, cols], wi_ref[h], preferred_element_type=F32))
    r = jax.nn.sigmoid(jnp.concatenate(r_parts, axis=1) + br_ref[...])
    ig = jax.nn.sigmoid(jnp.concatenate(i_parts, axis=1) + bi_ref[...])

    z = -lam_ref[...]
    softplus = jnp.maximum(z, 0.0) + jnp.log1p(jnp.exp(-jnp.abs(z)))
    log_a = (-LRU_C) * r * softplus
    a = jnp.exp(log_a)
    t = 1.0 - a * a
    b = jnp.where(t > 0.0, t * lax.rsqrt(t), 0.0) * (ig * xc)

    row = lax.broadcasted_iota(jnp.int32, (ts, 1), 0) & (SUBLANES - 1)
    d = 1
    while d < SUBLANES:
        keep = row >= d
        b = jnp.where(keep, a * pltpu.roll(b, d, 0) + b, b)
        a = jnp.where(keep, a * pltpu.roll(a, d, 0), a)
        d *= 2
    a_sc[...] = a
    b_sc[...] = b

    def group(gidx, carry):
        rows = pl.ds(pl.multiple_of(gidx * SUBLANES, SUBLANES), SUBLANES)
        h8 = b_sc[rows, :] + a_sc[rows, :] * carry
        b_sc[rows, :] = h8
        return h8[SUBLANES - 1:SUBLANES, :]

    carry_sc[...] = lax.fori_loop(0, ts // SUBLANES, group, carry_sc[...], unroll=8)
    o_ref[...] = (b_sc[...] * jax.nn.gelu(y_ref[...], approximate=True)).astype(BF16)


def _lru(lrux, lruy, cw, cb, wr, br, wi, bi, lam, layer):
    s = lrux.shape[0]
    ts = TS_LRU
    rows = pl.BlockSpec((ts, LRU_WIDTH), lambda i: (i, 0))
    vec = _resident((1, LRU_WIDTH))
    return pl.pallas_call(
        _lru_kernel,
        grid=(s // ts,),
        in_specs=[rows, rows, _resident((CONV_W, LRU_WIDTH)), vec, _layer_resident(wr, layer), vec,
                  _layer_resident(wi, layer), vec, vec],
        out_specs=rows,
        out_shape=jax.ShapeDtypeStruct((s, LRU_WIDTH), BF16),
        scratch_shapes=[pltpu.VMEM((ts + SUBLANES, LRU_WIDTH), F32),
                        pltpu.VMEM((ts, LRU_WIDTH), F32),
                        pltpu.VMEM((ts, LRU_WIDTH), F32),
                        pltpu.VMEM((1, LRU_WIDTH), F32)],
        compiler_params=_params("arbitrary"),
        name="rglru",
    )(lrux, lruy, cw, cb, wr, br, wi, bi, lam)


NEG = -1e30
LOG2_E = math.log2(math.e)


def _pair_tables(n_blk):
    qi, kj = [], []
    for i in range(n_blk):
        for j in range(i + 1):
            qi.append(i)
            kj.append(j)
    return jnp.asarray(qi, jnp.int32), jnp.asarray(kj, jnp.int32)


def _online_softmax_step(st, vt1, m_sc, acc_sc, idx, qs):
    m_prev = m_sc[idx, :, qs]
    m_new = jnp.maximum(m_prev, jnp.max(st, axis=0, keepdims=True))
    alpha = jnp.exp2(m_prev - m_new)
    pt = jnp.exp2((st - m_new).astype(BF16))
    acc_sc[idx, :, qs] = alpha * acc_sc[idx, :, qs] + jnp.dot(vt1, pt, preferred_element_type=F32)
    m_sc[idx, :, qs] = m_new


def _block_parts(diagonal):
    full, half = slice(0, T_ATT), T_ATT // 2
    if not diagonal:
        return [(full, full)]
    return [(slice(0, half), full), (slice(half, T_ATT), slice(half, T_ATT))]


def _kq(k, q):
    return lax.dot_general(k, q, (((1,), (1,)), ((), ())), preferred_element_type=F32)


def _chunk_mask(t):
    shift = CHUNK.bit_length() - 1
    key = lax.broadcasted_iota(jnp.int32, (t, t), 0) >> shift
    qry = lax.broadcasted_iota(jnp.int32, (t, t), 1) >> shift
    return key <= qry


def _pipelined_maps(groups, diagonal):
    steps = [(group, idx, ks, qs) for group in groups for ks, qs in _block_parts(diagonal)
             for idx in range(group[4])]
    pending = [group[0](idx, ks, qs) for group, idx, ks, qs in steps[:QK_AHEAD]]
    for n, (group, idx, ks, qs) in enumerate(steps):
        if n + QK_AHEAD < len(steps):
            nxt, nidx, nks, nqs = steps[n + QK_AHEAD]
            pending.append(nxt[0](nidx, nks, nqs))
        _, values, m_sc, acc_sc, _ = group
        _online_softmax_step(pending.pop(0), values(idx, ks), m_sc, acc_sc, idx, qs)


def _init_stats(m_sc, acc_sc):
    m_sc[...] = jnp.full(m_sc.shape, NEG, F32)
    acc_sc[...] = jnp.zeros(acc_sc.shape, F32)


def _normalized(acc_sc, idx):
    return acc_sc[idx, 0:V_DIM, :] / acc_sc[idx, V_DIM:V_DIM + 1, :]


def _attn_kernel(qi_ref, kj_ref, dq_ref, dk_ref, dvt_ref, lq1_ref, lk1_ref, lq2_ref, lk2_ref,
                 gsub_ref, linit_ref, mq_ref, mk_ref, mvt_ref, od_ref, om_ref,
                 dm_sc, dacc_sc, mm_sc, macc_sc):
    t = pl.program_id(0)
    i = qi_ref[t]
    j = kj_ref[t]

    @pl.when(j == 0)
    def _():
        _init_stats(dm_sc, dacc_sc)
        _init_stats(mm_sc, macc_sc)

    lane = lax.broadcasted_iota(jnp.int32, (1, LANES), 1)
    first_map = lane < DIFF_HEAD_DIM

    def block(diagonal):
        mask = _chunk_mask(T_ATT) if diagonal else None

        def masked(st, ks, qs):
            return jnp.where(mask[ks, qs], st, NEG) if diagonal else st

        def diff_scores(idx, ks, qs):
            h, m = divmod(idx, 2)
            cols = slice(h * LANES, (h + 1) * LANES)
            q = dq_ref[qs, cols]
            zero = jnp.zeros_like(q)
            qm = jnp.where(first_map, q, zero) if m == 0 else jnp.where(first_map, zero, q)
            return masked(_kq(dk_ref[ks, cols], qm), ks, qs)

        def mla_scores(h, ks, qs):
            return masked(_kq(mk_ref[h, ks, :], mq_ref[h, qs, :]), ks, qs)

        _pipelined_maps(
            [(diff_scores, lambda idx, ks: dvt_ref[idx // 2, :, ks], dm_sc, dacc_sc, 2 * DIFF_HEADS),
             (mla_scores, lambda h, ks: mvt_ref[h, :, ks], mm_sc, macc_sc, MLA_HEADS)], diagonal)

    @pl.when(j < i)
    def _():
        block(False)

    @pl.when(j == i)
    def _():
        block(True)
        linit = linit_ref[:, 0:1]
        lam = (jnp.exp(jnp.sum(lq1_ref[...] * lk1_ref[...], axis=1, keepdims=True))
               - jnp.exp(jnp.sum(lq2_ref[...] * lk2_ref[...], axis=1, keepdims=True)) + linit)
        for h in range(DIFF_HEADS):
            ot = _normalized(dacc_sc, 2 * h) - lam * _normalized(dacc_sc, 2 * h + 1)
            o = _rms(ot.T, gsub_ref[...], SUBLN_EPS) * (1.0 - linit)
            od_ref[:, h * LANES:(h + 1) * LANES] = o.astype(BF16)
        for h in range(MLA_HEADS):
            om_ref[:, h * LANES:(h + 1) * LANES] = _normalized(macc_sc, h).T.astype(BF16)


def _attention(dq, dk, dvt, lq1, lk1, lq2, lk2, gsub, linit, qm, km, vmt, qi, kj):
    s = dq.shape[0]
    t = T_ATT
    rows_q = lambda w: pl.BlockSpec((t, w), lambda p, qi, kj: (qi[p], 0))
    rows_k = lambda w: pl.BlockSpec((t, w), lambda p, qi, kj: (kj[p], 0))
    vt = lambda nh: pl.BlockSpec((nh, V_ROWS, t), lambda p, qi, kj: (0, 0, kj[p]))
    heads_q = pl.BlockSpec((MLA_HEADS, t, MLA_QK), lambda p, qi, kj: (0, qi[p], 0))
    heads_k = pl.BlockSpec((MLA_HEADS, t, MLA_QK), lambda p, qi, kj: (0, kj[p], 0))
    small = lambda w: pl.BlockSpec((1, w), lambda p, qi, kj: (0, 0))
    nm = 2 * DIFF_HEADS
    return pl.pallas_call(
        _attn_kernel,
        grid_spec=pltpu.PrefetchScalarGridSpec(
            num_scalar_prefetch=2,
            grid=(qi.shape[0],),
            in_specs=[rows_q(DIFF_QK), rows_k(DIFF_QK), vt(DIFF_HEADS), small(DIFF_HEAD_DIM),
                      small(DIFF_HEAD_DIM), small(DIFF_HEAD_DIM), small(DIFF_HEAD_DIM),
                      small(DIFF_V_DIM), small(LANES), heads_q, heads_k, vt(MLA_HEADS)],
            out_specs=[rows_q(DIFF_OUT), rows_q(MLA_OUT)],
            scratch_shapes=[pltpu.VMEM((nm, 1, t), F32), pltpu.VMEM((nm, V_ROWS, t), F32),
                            pltpu.VMEM((MLA_HEADS, 1, t), F32),
                            pltpu.VMEM((MLA_HEADS, V_ROWS, t), F32)]),
        out_shape=[jax.ShapeDtypeStruct((s, DIFF_OUT), BF16), jax.ShapeDtypeStruct((s, MLA_OUT), BF16)],
        compiler_params=_params("arbitrary"),
        name="attention",
    )(qi, kj, dq, dk, dvt, lq1, lk1, lq2, lk2, gsub, linit, qm, km, vmt)


def _outproj_kernel(x_ref, a_ref, b_ref, c_ref, w_ref, o_ref, wb_sc):
    @pl.when(pl.program_id(0) == 0)
    def _():
        wb_sc[...] = w_ref[...].astype(BF16)

    acc = jnp.dot(a_ref[...], wb_sc[0:LRU_WIDTH, :], preferred_element_type=F32)
    acc = acc + jnp.dot(b_ref[...], wb_sc[LRU_WIDTH:LRU_WIDTH + DIFF_OUT, :], preferred_element_type=F32)
    acc = acc + jnp.dot(c_ref[...], wb_sc[LRU_WIDTH + DIFF_OUT:D_MODEL, :], preferred_element_type=F32)
    o_ref[...] = x_ref[...] + acc


def _outproj(x, a, b, c, w, layer):
    s = x.shape[0]
    tm = TM_OUT
    rows = lambda wd: pl.BlockSpec((tm, wd), lambda i: (i, 0))
    return pl.pallas_call(
        _outproj_kernel,
        grid=(s // tm,),
        in_specs=[rows(D_MODEL), rows(LRU_WIDTH), rows(DIFF_OUT), rows(MLA_OUT),
                  _layer_resident(w, layer)],
        out_specs=rows(D_MODEL),
        out_shape=jax.ShapeDtypeStruct((s, D_MODEL), F32),
        scratch_shapes=[pltpu.VMEM((D_MODEL, D_MODEL), BF16)],
        compiler_params=_params("arbitrary"),
        name="outproj",
    )(x, a, b, c, w)


def _ffn_kernel(x_ref, g_ref, wg_ref, wu_ref, wd_ref, gf_ref, o_ref, hb_sc, *, final):
    f = pl.program_id(1)

    @pl.when(f == 0)
    def _():
        x = x_ref[...]
        hb_sc[...] = _rms(x, g_ref[...], NORM_EPS).astype(BF16)
        o_ref[...] = x

    hb = hb_sc[...]
    gate = jnp.dot(hb, wg_ref[...].astype(BF16), preferred_element_type=F32)
    up = jnp.dot(hb, wu_ref[...].astype(BF16), preferred_element_type=F32)
    act = (gate * jax.nn.sigmoid(gate) * up).astype(BF16)
    o_ref[...] += jnp.dot(act, wd_ref[...].astype(BF16), preferred_element_type=F32)

    if final:
        @pl.when(f == pl.num_programs(1) - 1)
        def _():
            o_ref[...] = _rms(o_ref[...], gf_ref[...], NORM_EPS)


def _ffn(x, g, wg, wu, wd, gf, layer, final):
    s = x.shape[0]
    tm, tf = TM_FFN, TF_FFN
    rows = pl.BlockSpec((tm, D_MODEL), lambda i, f: (i, 0))
    rows_in = pl.BlockSpec((tm, D_MODEL), lambda i, f: (i, 0), pipeline_mode=pl.Buffered(1))
    vec = pl.BlockSpec((1, D_MODEL), lambda i, f: (0, 0))
    return pl.pallas_call(
        functools.partial(_ffn_kernel, final=final),
        grid=(s // tm, D_FF // tf),
        in_specs=[rows_in, vec, pl.BlockSpec((None, D_MODEL, tf), lambda i, f: (layer, 0, f)),
                  pl.BlockSpec((None, D_MODEL, tf), lambda i, f: (layer, 0, f)),
                  pl.BlockSpec((None, tf, D_MODEL), lambda i, f: (layer, f, 0)), vec],
        out_specs=rows,
        out_shape=jax.ShapeDtypeStruct((s, D_MODEL), F32),
        scratch_shapes=[pltpu.VMEM((tm, D_MODEL), BF16)],
        compiler_params=_params("parallel", "arbitrary"),
        name="ffn",
    )(x, g, wg, wu, wd, gf)


def kernel(x, positions, g_mix, w_in, conv_w, conv_b, w_r, b_r, w_i, b_i, lru_lambda, lam_q1, lam_k1,
           lam_q2, lam_k2, g_sub, g_q_a, w_q_b, g_kv_a, w_kv_b, w_out, g_ffn, w_gate, w_up, w_down,
           g_final):
    batch, seq, _ = x.shape
    assert batch == 1 and seq % TM_FFN == 0 and seq % T_ATT == 0
    xs = x.reshape(seq, D_MODEL)
    pos_col = positions.reshape(seq, 1)

    dtab = _rope_tables(pos_col, DIFF_HEAD_DIM, DIFF_ROT)
    mtab = _rope_tables(pos_col, LANES, MLA_ROPE)
    qi, kj = _pair_tables(seq // T_ATT)

    win = w_in.astype(BF16)
    wqb = w_q_b.astype(BF16)
    wkvb = w_kv_b.astype(BF16)
    wr = w_r.astype(BF16)
    wi = w_i.astype(BF16)
    row = lambda v: v.reshape(1, -1)

    for l in range(DEPTH):
        lambda_init = 0.8 - 0.6 * math.exp(-0.3 * l)
        linit = jnp.full((1, LANES), lambda_init, F32)
        lrux, lruy, dq, dk, dvt, qm, km, vmt = _inproj(
            xs, row(g_mix[l]), win, wqb, wkvb, row(g_q_a[l]), row(g_kv_a[l]), dtab, mtab, l)
        out_a = _lru(lrux, lruy, conv_w[l], row(conv_b[l]), wr, row(b_r[l]), wi, row(b_i[l]),
                     row(lru_lambda[l]), l)
        out_b, out_c = _attention(dq, dk, dvt, row(lam_q1[l]), row(lam_k1[l]), row(lam_q2[l]),
                                  row(lam_k2[l]), row(g_sub[l]), linit, qm, km, vmt, qi, kj)
        xs = _outproj(xs, out_a, out_b, out_c, w_out, l)
        xs = _ffn(xs, row(g_ffn[l]), w_gate, w_up, w_down, row(g_final), l, final=(l == DEPTH - 1))
    return xs.reshape(batch, seq, D_MODEL)
```

```python
import functools
import math

import jax
import jax.numpy as jnp
import numpy as np
from jax import lax
from jax.experimental import pallas as pl
from jax.experimental.pallas import tpu as pltpu

D_MODEL = 2048
SEQ = 8192
DEPTH = 4
CHUNK = 64
ROPE_THETA = 500000.0
NORM_EPS = 1e-6

LRU_WIDTH = 768
LRU_BLOCKS = 6
LRU_BLOCK_W = 128
CONV_W = 4
LRU_C = 8.0

DIFF_HEADS = 4
DIFF_HEAD_DIM = 64
DIFF_V_DIM = 128
DIFF_QK = 512
DIFF_OUT = 512
DIFF_ROT = 16
SUBLN_EPS = 1e-5

MLA_HEADS = 6
MLA_NOPE = 128
MLA_ROPE = 64
MLA_V = 128
MLA_QK = MLA_NOPE + MLA_ROPE
MLA_Q_RANK = 512
MLA_KV_RANK = 256
MLA_OUT = 768

IN_WIDTH = 3904
D_FF = 5632

LANES = 128
SUBLANES = 8
VMEM_LIMIT = 60 * 1024 * 1024

TM_PROJ = 512
TS_LRU = 512
T_ATT = 1024
QK_AHEAD = 1
V_DIM = 128
BF16_ROWS = 16
V_ROWS = V_DIM + BF16_ROWS
TM_OUT = 512
TM_FFN = 1024
TF_FFN = 512

F32 = jnp.float32
BF16 = jnp.bfloat16


def _params(*sem):
    return pltpu.CompilerParams(dimension_semantics=sem, vmem_limit_bytes=VMEM_LIMIT)


def _resident(shape):
    nd = len(shape)
    return pl.BlockSpec(shape, lambda *_: (0,) * nd, pipeline_mode=pl.Buffered(1))


def _layer_resident(stacked, layer):
    nd = stacked.ndim - 1
    return pl.BlockSpec((None,) + stacked.shape[1:], lambda *_: (layer,) + (0,) * nd,
                        pipeline_mode=pl.Buffered(1))


def _rms(x, g, eps):
    return x * lax.rsqrt(jnp.mean(x * x, axis=-1, keepdims=True) + eps) * g


def _rope_table_kernel(pos_ref, invf_ref, lo_ref, hi_ref, c_ref, sp_ref, sm_ref):
    ang = pos_ref[...].astype(F32) * invf_ref[...]
    lo = lo_ref[...]
    hi = hi_ref[...]
    c_ref[...] = jnp.where(lo + hi > 0.0, jnp.cos(ang), 1.0)
    s = jnp.sin(ang)
    sp_ref[...] = s * hi
    sm_ref[...] = -s * lo


def _rope_tables(pos_col, group, rot_dim):
    half = rot_dim // 2
    inv_freq = ROPE_THETA ** (-jnp.arange(half, dtype=F32) / half)
    lane = np.arange(LANES) % group
    sel = jnp.asarray(lane % half)
    rot = lane < rot_dim
    invf = jnp.where(jnp.asarray(rot), inv_freq[sel], 0.0).reshape(1, LANES)
    lo = jnp.asarray((lane < half).astype(np.float32)).reshape(1, LANES)
    hi = jnp.asarray(((lane >= half) & rot).astype(np.float32)).reshape(1, LANES)
    s = pos_col.shape[0]
    tm = 1024
    row = pl.BlockSpec((1, LANES), lambda i: (0, 0))
    out = pl.BlockSpec((tm, LANES), lambda i: (i, 0))
    return pl.pallas_call(
        _rope_table_kernel,
        grid=(s // tm,),
        in_specs=[pl.BlockSpec((tm, 1), lambda i: (i, 0)), row, row, row],
        out_specs=[out, out, out],
        out_shape=[jax.ShapeDtypeStruct((s, LANES), F32)] * 3,
        compiler_params=_params("parallel"),
        name="rope_tables",
    )(pos_col, invf, lo, hi)


def _rope_cols(x, c, sp, sm, half):
    return x * c + pltpu.roll(x, half, 1) * sp + pltpu.roll(x, LANES - half, 1) * sm


def _rope_tail(x, c, sp, sm):
    x = pltpu.roll(x, MLA_ROPE, 1)
    return _rope_cols(x, c, sp, sm, MLA_ROPE // 2)[:, :MLA_ROPE]


def _inproj_kernel(x_ref, g_ref, win_ref, wqb_ref, wkvb_ref, gqa_ref, gkva_ref,
                   dc_ref, dsp_ref, dsm_ref, mc_ref, msp_ref, msm_ref,
                   lrux_ref, lruy_ref, dq_ref, dk_ref, dvt_ref, qm_ref, km_ref, vmt_ref):
    hb = _rms(x_ref[...], g_ref[...], NORM_EPS).astype(BF16)

    def proj(lo, hi):
        return jnp.dot(hb, win_ref[:, lo:hi], preferred_element_type=F32)

    mc, msp, msm = mc_ref[...], msp_ref[...], msm_ref[...]
    qa = proj(3072, 3584)
    qn = _rms(qa, gqa_ref[...], NORM_EPS).astype(BF16)
    kva = proj(3584, IN_WIDTH)
    kr = _rope_tail(kva[:, IN_WIDTH - 3584 - LANES:], mc, msp, msm).astype(BF16)
    kvn = _rms(kva[:, :MLA_KV_RANK], gkva_ref[...], NORM_EPS).astype(BF16)

    dc, dsp, dsm = dc_ref[...], dsp_ref[...], dsm_ref[...]
    diff_scale = DIFF_HEAD_DIM ** -0.5 * LOG2_E
    dq = proj(1536, 2048)
    for h in range(DIFF_HEADS):
        cols = slice(h * LANES, (h + 1) * LANES)
        dq_ref[:, cols] = (_rope_cols(dq[:, cols], dc, dsp, dsm, DIFF_ROT // 2) * diff_scale).astype(BF16)

    q = jnp.dot(qn, wqb_ref[...], preferred_element_type=F32)
    mla_scale = MLA_QK ** -0.5 * LOG2_E
    for h in range(MLA_HEADS):
        base = h * MLA_QK
        qm_ref[h, :, 0:MLA_NOPE] = (q[:, base:base + MLA_NOPE] * mla_scale).astype(BF16)
        qr = _rope_tail(q[:, base + MLA_QK - LANES:base + MLA_QK], mc, msp, msm)
        qm_ref[h, :, MLA_NOPE:MLA_QK] = (qr * mla_scale).astype(BF16)

    dk = proj(2048, 2560)
    for h in range(DIFF_HEADS):
        cols = slice(h * LANES, (h + 1) * LANES)
        dk_ref[:, cols] = _rope_cols(dk[:, cols], dc, dsp, dsm, DIFF_ROT // 2).astype(BF16)

    ones = jnp.ones((BF16_ROWS, x_ref.shape[0]), BF16)
    kv = jnp.dot(kvn, wkvb_ref[...], preferred_element_type=F32)
    for h in range(MLA_HEADS):
        base = h * 2 * LANES
        km_ref[h, :, 0:MLA_NOPE] = kv[:, base:base + LANES].astype(BF16)
        km_ref[h, :, MLA_NOPE:MLA_QK] = kr
        vmt_ref[h, 0:V_DIM, :] = kv[:, base + LANES:base + 2 * LANES].T.astype(BF16)
        vmt_ref[h, V_DIM:V_ROWS, :] = ones

    dv = proj(2560, 3072)
    for h in range(DIFF_HEADS):
        dvt_ref[h, 0:V_DIM, :] = dv[:, h * LANES:(h + 1) * LANES].T.astype(BF16)
        dvt_ref[h, V_DIM:V_ROWS, :] = ones

    lrux_ref[...] = proj(0, 768)
    lruy_ref[...] = proj(768, 1536)


def _inproj(x, g, win, wqb, wkvb, gqa, gkva, dtab, mtab, layer):
    s = x.shape[0]
    tm = TM_PROJ
    rows = lambda w: pl.BlockSpec((tm, w), lambda i: (i, 0))
    heads = pl.BlockSpec((MLA_HEADS, tm, MLA_QK), lambda i: (0, i, 0))
    vt = lambda nh: pl.BlockSpec((nh, V_ROWS, tm), lambda i: (0, 0, i))
    tab = rows(LANES)
    return pl.pallas_call(
        _inproj_kernel,
        grid=(s // tm,),
        in_specs=[rows(D_MODEL), _resident((1, D_MODEL)), _layer_resident(win, layer),
                  _layer_resident(wqb, layer), _layer_resident(wkvb, layer),
                  _resident((1, MLA_Q_RANK)), _resident((1, MLA_KV_RANK)),
                  tab, tab, tab, tab, tab, tab],
        out_specs=[rows(LRU_WIDTH), rows(LRU_WIDTH), rows(DIFF_QK), rows(DIFF_QK), vt(DIFF_HEADS),
                   heads, heads, vt(MLA_HEADS)],
        out_shape=[jax.ShapeDtypeStruct((s, LRU_WIDTH), F32), jax.ShapeDtypeStruct((s, LRU_WIDTH), F32),
                   jax.ShapeDtypeStruct((s, DIFF_QK), BF16), jax.ShapeDtypeStruct((s, DIFF_QK), BF16),
                   jax.ShapeDtypeStruct((DIFF_HEADS, V_ROWS, s), BF16),
                   jax.ShapeDtypeStruct((MLA_HEADS, s, MLA_QK), BF16),
                   jax.ShapeDtypeStruct((MLA_HEADS, s, MLA_QK), BF16),
                   jax.ShapeDtypeStruct((MLA_HEADS, V_ROWS, s), BF16)],
        compiler_params=_params("parallel"),
        name="inproj",
    )(x, g, win, wqb, wkvb, gqa, gkva, *dtab, *mtab)


def _lru_kernel(x_ref, y_ref, cw_ref, cb_ref, wr_ref, br_ref, wi_ref, bi_ref, lam_ref, o_ref,
                xext_sc, a_sc, b_sc, carry_sc):
    ts = x_ref.shape[0]
    pad = SUBLANES

    @pl.when(pl.program_id(0) == 0)
    def _():
        xext_sc[0:pad, :] = jnp.zeros((pad, LRU_WIDTH), F32)
        carry_sc[...] = jnp.zeros_like(carry_sc)

    xext_sc[pad:pad + ts, :] = x_ref[...]
    xc = cb_ref[...] + cw_ref[CONV_W - 1:CONV_W, :] * xext_sc[pad:pad + ts, :]
    for k in range(1, CONV_W):
        xc = xc + cw_ref[CONV_W - 1 - k:CONV_W - k, :] * xext_sc[pad - k:pad - k + ts, :]
    xext_sc[0:pad, :] = x_ref[ts - pad:ts, :]

    xcb = xc.astype(BF16)
    r_parts, i_parts = [], []
    for h in range(LRU_BLOCKS):
        cols = slice(h * LRU_BLOCK_W, (h + 1) * LRU_BLOCK_W)
        r_parts.append(jnp.dot(xcb[:, cols], wr_ref[h], preferred_element_type=F32))
        i_parts.append(jnp.dot(xcb[:, cols], wi_ref[h], preferred_element_type=F32))
    r = jax.nn.sigmoid(jnp.concatenate(r_parts, axis=1) + br_ref[...])
    ig = jax.nn.sigmoid(jnp.concatenate(i_parts, axis=1) + bi_ref[...])

    z = -lam_ref[...]
    softplus = jnp.maximum(z, 0.0) + jnp.log1p(jnp.exp(-jnp.abs(z)))
    log_a = (-LRU_C) * r * softplus
    a = jnp.exp(log_a)
    t = 1.0 - a * a
    b = jnp.where(t > 0.0, t * lax.rsqrt(t), 0.0) * (ig * xc)

    row = lax.broadcasted_iota(jnp.int32, (ts, 1), 0) & (SUBLANES - 1)
    d = 1
    while d < SUBLANES:
        keep = row >= d
        b = jnp.where(keep, a * pltpu.roll(b, d, 0) + b, b)
        a = jnp.where(keep, a * pltpu.roll(a, d, 0), a)
        d *= 2
    a_sc[...] = a
    b_sc[...] = b

    def group(gidx, carry):
        rows = pl.ds(pl.multiple_of(gidx * SUBLANES, SUBLANES), SUBLANES)
        h8 = b_sc[rows, :] + a_sc[rows, :] * carry
        b_sc[rows, :] = h8
        return h8[SUBLANES - 1:SUBLANES, :]

    carry_sc[...] = lax.fori_loop(0, ts // SUBLANES, group, carry_sc[...], unroll=8)
    o_ref[...] = (b_sc[...] * jax.nn.gelu(y_ref[...], approximate=True)).astype(BF16)


def _lru(lrux, lruy, cw, cb, wr, br, wi, bi, lam, layer):
    s = lrux.shape[0]
    ts = TS_LRU
    rows = pl.BlockSpec((ts, LRU_WIDTH), lambda i: (i, 0))
    vec = _resident((1, LRU_WIDTH))
    return pl.pallas_call(
        _lru_kernel,
        grid=(s // ts,),
        in_specs=[rows, rows, _resident((CONV_W, LRU_WIDTH)), vec, _layer_resident(wr, layer), vec,
                  _layer_resident(wi, layer), vec, vec],
        out_specs=rows,
        out_shape=jax.ShapeDtypeStruct((s, LRU_WIDTH), BF16),
        scratch_shapes=[pltpu.VMEM((ts + SUBLANES, LRU_WIDTH), F32),
                        pltpu.VMEM((ts, LRU_WIDTH), F32),
                        pltpu.VMEM((ts, LRU_WIDTH), F32),
                        pltpu.VMEM((1, LRU_WIDTH), F32)],
        compiler_params=_params("arbitrary"),
        name="rglru",
    )(lrux, lruy, cw, cb, wr, br, wi, bi, lam)


NEG = -1e30
LOG2_E = math.log2(math.e)


def _pair_tables(n_blk):
    qi, kj = [], []
    for i in range(n_blk):
        for j in range(i + 1):
            qi.append(i)
            kj.append(j)
    return jnp.asarray(qi, jnp.int32), jnp.asarray(kj, jnp.int32)


def _online_softmax_step(st, vt1, m_sc, acc_sc, idx, qs):
    m_prev = m_sc[idx, :, qs]
    m_new = jnp.maximum(m_prev, jnp.max(st, axis=0, keepdims=True))
    alpha = jnp.exp2(m_prev - m_new)
    pt = jnp.exp2((st - m_new).astype(BF16))
    acc_sc[idx, :, qs] = alpha * acc_sc[idx, :, qs] + jnp.dot(vt1, pt, preferred_element_type=F32)
    m_sc[idx, :, qs] = m_new


def _block_parts(diagonal):
    full, half = slice(0, T_ATT), T_ATT // 2
    if not diagonal:
        return [(full, full)]
    return [(slice(0, half), full), (slice(half, T_ATT), slice(half, T_ATT))]


def _kq(k, q):
    return lax.dot_general(k, q, (((1,), (1,)), ((), ())), preferred_element_type=F32)


def _chunk_mask(t):
    shift = CHUNK.bit_length() - 1
    key = lax.broadcasted_iota(jnp.int32, (t, t), 0) >> shift
    qry = lax.broadcasted_iota(jnp.int32, (t, t), 1) >> shift
    return key <= qry


def _pipelined_maps(scores, values, m_sc, acc_sc, n_maps, diagonal):
    steps = [(idx, ks, qs) for ks, qs in _block_parts(diagonal) for idx in range(n_maps)]
    pending = [scores(*step) for step in steps[:QK_AHEAD]]
    for n, (idx, ks, qs) in enumerate(steps):
        if n + QK_AHEAD < len(steps):
            pending.append(scores(*steps[n + QK_AHEAD]))
        _online_softmax_step(pending.pop(0), values(idx, ks), m_sc, acc_sc, idx, qs)


def _init_stats(m_sc, acc_sc):
    m_sc[...] = jnp.full(m_sc.shape, NEG, F32)
    acc_sc[...] = jnp.zeros(acc_sc.shape, F32)


def _normalized(acc_sc, idx):
    return acc_sc[idx, 0:V_DIM, :] / acc_sc[idx, V_DIM:V_DIM + 1, :]


def _diff_attn_kernel(qi_ref, kj_ref, q_ref, k_ref, vt_ref, lq1_ref, lk1_ref, lq2_ref, lk2_ref,
                      gsub_ref, linit_ref, o_ref, m_sc, acc_sc):
    t = pl.program_id(0)
    i = qi_ref[t]
    j = kj_ref[t]

    @pl.when(j == 0)
    def _():
        _init_stats(m_sc, acc_sc)

    lane = lax.broadcasted_iota(jnp.int32, (1, LANES), 1)
    first_map = lane < DIFF_HEAD_DIM

    def block(diagonal):
        mask = _chunk_mask(T_ATT) if diagonal else None

        def scores(idx, ks, qs):
            h, m = divmod(idx, 2)
            cols = slice(h * LANES, (h + 1) * LANES)
            q = q_ref[qs, cols]
            zero = jnp.zeros_like(q)
            qm = jnp.where(first_map, q, zero) if m == 0 else jnp.where(first_map, zero, q)
            st = _kq(k_ref[ks, cols], qm)
            return jnp.where(mask[ks, qs], st, NEG) if diagonal else st

        _pipelined_maps(scores, lambda idx, ks: vt_ref[idx // 2, :, ks], m_sc, acc_sc,
                        2 * DIFF_HEADS, diagonal)

    @pl.when(j < i)
    def _():
        block(False)

    @pl.when(j == i)
    def _():
        block(True)
        linit = linit_ref[:, 0:1]
        lam = (jnp.exp(jnp.sum(lq1_ref[...] * lk1_ref[...], axis=1, keepdims=True))
               - jnp.exp(jnp.sum(lq2_ref[...] * lk2_ref[...], axis=1, keepdims=True)) + linit)
        for h in range(DIFF_HEADS):
            ot = _normalized(acc_sc, 2 * h) - lam * _normalized(acc_sc, 2 * h + 1)
            o = _rms(ot.T, gsub_ref[...], SUBLN_EPS) * (1.0 - linit)
            o_ref[:, h * LANES:(h + 1) * LANES] = o.astype(BF16)


def _diff_attention(dq, dk, dvt, lq1, lk1, lq2, lk2, gsub, linit, qi, kj):
    s = dq.shape[0]
    t = T_ATT
    qspec = pl.BlockSpec((t, DIFF_QK), lambda p, qi, kj: (qi[p], 0))
    kspec = pl.BlockSpec((t, DIFF_QK), lambda p, qi, kj: (kj[p], 0))
    vspec = pl.BlockSpec((DIFF_HEADS, V_ROWS, t), lambda p, qi, kj: (0, 0, kj[p]))
    small = lambda w: pl.BlockSpec((1, w), lambda p, qi, kj: (0, 0))
    nm = 2 * DIFF_HEADS
    return pl.pallas_call(
        _diff_attn_kernel,
        grid_spec=pltpu.PrefetchScalarGridSpec(
            num_scalar_prefetch=2,
            grid=(qi.shape[0],),
            in_specs=[qspec, kspec, vspec, small(DIFF_HEAD_DIM), small(DIFF_HEAD_DIM),
                      small(DIFF_HEAD_DIM), small(DIFF_HEAD_DIM), small(DIFF_V_DIM), small(LANES)],
            out_specs=qspec,
            scratch_shapes=[pltpu.VMEM((nm, 1, t), F32), pltpu.VMEM((nm, V_ROWS, t), F32)]),
        out_shape=jax.ShapeDtypeStruct((s, DIFF_OUT), BF16),
        compiler_params=_params("arbitrary"),
        name="diff_attention",
    )(qi, kj, dq, dk, dvt, lq1, lk1, lq2, lk2, gsub, linit)


def _mla_attn_kernel(qi_ref, kj_ref, q_ref, k_ref, vt_ref, o_ref, m_sc, acc_sc):
    t = pl.program_id(0)
    i = qi_ref[t]
    j = kj_ref[t]

    @pl.when(j == 0)
    def _():
        _init_stats(m_sc, acc_sc)

    def block(diagonal):
        mask = _chunk_mask(T_ATT) if diagonal else None

        def scores(h, ks, qs):
            st = _kq(k_ref[h, ks, :], q_ref[h, qs, :])
            return jnp.where(mask[ks, qs], st, NEG) if diagonal else st

        _pipelined_maps(scores, lambda h, ks: vt_ref[h, :, ks], m_sc, acc_sc, MLA_HEADS, diagonal)

    @pl.when(j < i)
    def _():
        block(False)

    @pl.when(j == i)
    def _():
        block(True)
        for h in range(MLA_HEADS):
            o_ref[:, h * LANES:(h + 1) * LANES] = _normalized(acc_sc, h).T.astype(BF16)


def _mla_attention(qm, km, vmt, qi, kj):
    s = qm.shape[1]
    t = T_ATT
    qspec = pl.BlockSpec((MLA_HEADS, t, MLA_QK), lambda p, qi, kj: (0, qi[p], 0))
    kspec = pl.BlockSpec((MLA_HEADS, t, MLA_QK), lambda p, qi, kj: (0, kj[p], 0))
    vspec = pl.BlockSpec((MLA_HEADS, V_ROWS, t), lambda p, qi, kj: (0, 0, kj[p]))
    ospec = pl.BlockSpec((t, MLA_OUT), lambda p, qi, kj: (qi[p], 0))
    return pl.pallas_call(
        _mla_attn_kernel,
        grid_spec=pltpu.PrefetchScalarGridSpec(
            num_scalar_prefetch=2,
            grid=(qi.shape[0],),
            in_specs=[qspec, kspec, vspec],
            out_specs=ospec,
            scratch_shapes=[pltpu.VMEM((MLA_HEADS, 1, t), F32), pltpu.VMEM((MLA_HEADS, V_ROWS, t), F32)]),
        out_shape=jax.ShapeDtypeStruct((s, MLA_OUT), BF16),
        compiler_params=_params("arbitrary"),
        name="mla_attention",
    )(qi, kj, qm, km, vmt)


def _outproj_kernel(x_ref, a_ref, b_ref, c_ref, w_ref, o_ref, wb_sc):
    @pl.when(pl.program_id(0) == 0)
    def _():
        wb_sc[...] = w_ref[...].astype(BF16)

    acc = jnp.dot(a_ref[...], wb_sc[0:LRU_WIDTH, :], preferred_element_type=F32)
    acc = acc + jnp.dot(b_ref[...], wb_sc[LRU_WIDTH:LRU_WIDTH + DIFF_OUT, :], preferred_element_type=F32)
    acc = acc + jnp.dot(c_ref[...], wb_sc[LRU_WIDTH + DIFF_OUT:D_MODEL, :], preferred_element_type=F32)
    o_ref[...] = x_ref[...] + acc


def _outproj(x, a, b, c, w, layer):
    s = x.shape[0]
    tm = TM_OUT
    rows = lambda wd: pl.BlockSpec((tm, wd), lambda i: (i, 0))
    return pl.pallas_call(
        _outproj_kernel,
        grid=(s // tm,),
        in_specs=[rows(D_MODEL), rows(LRU_WIDTH), rows(DIFF_OUT), rows(MLA_OUT),
                  _layer_resident(w, layer)],
        out_specs=rows(D_MODEL),
        out_shape=jax.ShapeDtypeStruct((s, D_MODEL), F32),
        scratch_shapes=[pltpu.VMEM((D_MODEL, D_MODEL), BF16)],
        compiler_params=_params("arbitrary"),
        name="outproj",
    )(x, a, b, c, w)


def _ffn_kernel(x_hbm, g_ref, wg_ref, wu_ref, wd_ref, gf_ref, o_ref, x_sc, hb_sc, x_sem, *, final):
    i = pl.program_id(0)
    f = pl.program_id(1)
    tm = x_sc.shape[0]

    def x_copy(tile):
        return pltpu.make_async_copy(x_hbm.at[pl.ds(tile * tm, tm), :], x_sc, x_sem)

    @pl.when((i == 0) & (f == 0))
    def _():
        x_copy(0).start()

    @pl.when(f == 0)
    def _():
        x_copy(i).wait()
        x = x_sc[...]
        hb_sc[...] = _rms(x, g_ref[...], NORM_EPS).astype(BF16)
        o_ref[...] = x

    @pl.when((f == 1) & (i + 1 < pl.num_programs(0)))
    def _():
        x_copy(i + 1).start()

    hb = hb_sc[...]
    gate = jnp.dot(hb, wg_ref[...].astype(BF16), preferred_element_type=F32)
    up = jnp.dot(hb, wu_ref[...].astype(BF16), preferred_element_type=F32)
    act = (gate * jax.nn.sigmoid(gate) * up).astype(BF16)
    o_ref[...] += jnp.dot(act, wd_ref[...].astype(BF16), preferred_element_type=F32)

    if final:
        @pl.when(f == pl.num_programs(1) - 1)
        def _():
            o_ref[...] = _rms(o_ref[...], gf_ref[...], NORM_EPS)


def _ffn(x, g, wg, wu, wd, gf, layer, final):
    s = x.shape[0]
    tm, tf = TM_FFN, TF_FFN
    assert D_FF // tf >= 2
    rows = pl.BlockSpec((tm, D_MODEL), lambda i, f: (i, 0))
    vec = pl.BlockSpec((1, D_MODEL), lambda i, f: (0, 0))
    return pl.pallas_call(
        functools.partial(_ffn_kernel, final=final),
        grid=(s // tm, D_FF // tf),
        in_specs=[pl.BlockSpec(memory_space=pl.ANY), vec,
                  pl.BlockSpec((None, D_MODEL, tf), lambda i, f: (layer, 0, f)),
                  pl.BlockSpec((None, D_MODEL, tf), lambda i, f: (layer, 0, f)),
                  pl.BlockSpec((None, tf, D_MODEL), lambda i, f: (layer, f, 0)), vec],
        out_specs=rows,
        out_shape=jax.ShapeDtypeStruct((s, D_MODEL), F32),
        scratch_shapes=[pltpu.VMEM((tm, D_MODEL), F32), pltpu.VMEM((tm, D_MODEL), BF16),
                        pltpu.SemaphoreType.DMA(())],
        compiler_params=_params("arbitrary", "arbitrary"),
        name="ffn",
    )(x, g, wg, wu, wd, gf)


def kernel(x, positions, g_mix, w_in, conv_w, conv_b, w_r, b_r, w_i, b_i, lru_lambda, lam_q1, lam_k1,
           lam_q2, lam_k2, g_sub, g_q_a, w_q_b, g_kv_a, w_kv_b, w_out, g_ffn, w_gate, w_up, w_down,
           g_final):
    batch, seq, _ = x.shape
    assert batch == 1 and seq % TM_FFN == 0 and seq % T_ATT == 0
    xs = x.reshape(seq, D_MODEL)
    pos_col = positions.reshape(seq, 1)

    dtab = _rope_tables(pos_col, DIFF_HEAD_DIM, DIFF_ROT)
    mtab = _rope_tables(pos_col, LANES, MLA_ROPE)
    qi, kj = _pair_tables(seq // T_ATT)

    win = w_in.astype(BF16)
    wqb = w_q_b.astype(BF16)
    wkvb = w_kv_b.astype(BF16)
    wr = w_r.astype(BF16)
    wi = w_i.astype(BF16)
    row = lambda v: v.reshape(1, -1)

    for l in range(DEPTH):
        lambda_init = 0.8 - 0.6 * math.exp(-0.3 * l)
        linit = jnp.full((1, LANES), lambda_init, F32)
        lrux, lruy, dq, dk, dvt, qm, km, vmt = _inproj(
            xs, row(g_mix[l]), win, wqb, wkvb, row(g_q_a[l]), row(g_kv_a[l]), dtab, mtab, l)
        out_a = _lru(lrux, lruy, conv_w[l], row(conv_b[l]), wr, row(b_r[l]), wi, row(b_i[l]),
                     row(lru_lambda[l]), l)
        out_b = _diff_attention(dq, dk, dvt, row(lam_q1[l]), row(lam_k1[l]), row(lam_q2[l]),
                                row(lam_k2[l]), row(g_sub[l]), linit, qi, kj)
        out_c = _mla_attention(qm, km, vmt, qi, kj)
        xs = _outproj(xs, out_a, out_b, out_c, w_out, l)
        xs = _ffn(xs, row(g_ffn[l]), w_gate, w_up, w_down, row(g_final), l, final=(l == DEPTH - 1))
    return xs.reshape(batch, seq, D_MODEL)
```

```python
import functools
import math

import jax
import jax.numpy as jnp
import numpy as np
from jax import lax
from jax.experimental import pallas as pl
from jax.experimental.pallas import tpu as pltpu

D_MODEL = 2048
SEQ = 8192
DEPTH = 4
CHUNK = 64
ROPE_THETA = 500000.0
NORM_EPS = 1e-6

LRU_WIDTH = 768
LRU_BLOCKS = 6
LRU_BLOCK_W = 128
CONV_W = 4
LRU_C = 8.0

DIFF_HEADS = 4
DIFF_HEAD_DIM = 64
DIFF_V_DIM = 128
DIFF_QK = 512
DIFF_OUT = 512
DIFF_ROT = 16
SUBLN_EPS = 1e-5

MLA_HEADS = 6
MLA_NOPE = 128
MLA_ROPE = 64
MLA_V = 128
MLA_QK = MLA_NOPE + MLA_ROPE
MLA_Q_RANK = 512
MLA_KV_RANK = 256
MLA_OUT = 768

IN_WIDTH = 3904
D_FF = 5632

LANES = 128
SUBLANES = 8
VMEM_LIMIT = 60 * 1024 * 1024

TM_PROJ = 512
TS_LRU = 512
T_ATT = 1024
QK_AHEAD = 1
V_DIM = 128
BF16_ROWS = 16
V_ROWS = V_DIM + BF16_ROWS
TM_OUT = 512
TM_FFN = 1024
TF_FFN = 512

F32 = jnp.float32
BF16 = jnp.bfloat16


def _params(*sem):
    return pltpu.CompilerParams(dimension_semantics=sem, vmem_limit_bytes=VMEM_LIMIT)


def _resident(shape):
    nd = len(shape)
    return pl.BlockSpec(shape, lambda *_: (0,) * nd, pipeline_mode=pl.Buffered(1))


def _layer_resident(stacked, layer):
    nd = stacked.ndim - 1
    return pl.BlockSpec((None,) + stacked.shape[1:], lambda *_: (layer,) + (0,) * nd,
                        pipeline_mode=pl.Buffered(1))


def _rms(x, g, eps):
    return x * lax.rsqrt(jnp.mean(x * x, axis=-1, keepdims=True) + eps) * g


def _rope_table_kernel(pos_ref, invf_ref, lo_ref, hi_ref, c_ref, sp_ref, sm_ref):
    ang = pos_ref[...].astype(F32) * invf_ref[...]
    lo = lo_ref[...]
    hi = hi_ref[...]
    c_ref[...] = jnp.where(lo + hi > 0.0, jnp.cos(ang), 1.0)
    s = jnp.sin(ang)
    sp_ref[...] = s * hi
    sm_ref[...] = -s * lo


def _rope_tables(pos_col, group, rot_dim):
    half = rot_dim // 2
    inv_freq = ROPE_THETA ** (-jnp.arange(half, dtype=F32) / half)
    lane = np.arange(LANES) % group
    sel = jnp.asarray(lane % half)
    rot = lane < rot_dim
    invf = jnp.where(jnp.asarray(rot), inv_freq[sel], 0.0).reshape(1, LANES)
    lo = jnp.asarray((lane < half).astype(np.float32)).reshape(1, LANES)
    hi = jnp.asarray(((lane >= half) & rot).astype(np.float32)).reshape(1, LANES)
    s = pos_col.shape[0]
    tm = 1024
    row = pl.BlockSpec((1, LANES), lambda i: (0, 0))
    out = pl.BlockSpec((tm, LANES), lambda i: (i, 0))
    return pl.pallas_call(
        _rope_table_kernel,
        grid=(s // tm,),
        in_specs=[pl.BlockSpec((tm, 1), lambda i: (i, 0)), row, row, row],
        out_specs=[out, out, out],
        out_shape=[jax.ShapeDtypeStruct((s, LANES), F32)] * 3,
        compiler_params=_params("parallel"),
        name="rope_tables",
    )(pos_col, invf, lo, hi)


def _rope_cols(x, c, sp, sm, half):
    return x * c + pltpu.roll(x, half, 1) * sp + pltpu.roll(x, LANES - half, 1) * sm


def _rope_tail(x, c, sp, sm):
    x = pltpu.roll(x, MLA_ROPE, 1)
    return _rope_cols(x, c, sp, sm, MLA_ROPE // 2)[:, :MLA_ROPE]


def _inproj_kernel(x_ref, g_ref, win_ref, wqb_ref, wkvb_ref, gqa_ref, gkva_ref,
                   dc_ref, dsp_ref, dsm_ref, mc_ref, msp_ref, msm_ref,
                   lrux_ref, lruy_ref, dq_ref, dk_ref, dvt_ref, qm_ref, km_ref, vmt_ref):
    hb = _rms(x_ref[...], g_ref[...], NORM_EPS).astype(BF16)

    def proj(lo, hi):
        return jnp.dot(hb, win_ref[:, lo:hi], preferred_element_type=F32)

    mc, msp, msm = mc_ref[...], msp_ref[...], msm_ref[...]
    qa = proj(3072, 3584)
    qn = _rms(qa, gqa_ref[...], NORM_EPS).astype(BF16)
    kva = proj(3584, IN_WIDTH)
    kr = _rope_tail(kva[:, IN_WIDTH - 3584 - LANES:], mc, msp, msm).astype(BF16)
    kvn = _rms(kva[:, :MLA_KV_RANK], gkva_ref[...], NORM_EPS).astype(BF16)

    dc, dsp, dsm = dc_ref[...], dsp_ref[...], dsm_ref[...]
    diff_scale = DIFF_HEAD_DIM ** -0.5 * LOG2_E
    dq = proj(1536, 2048)
    for h in range(DIFF_HEADS):
        cols = slice(h * LANES, (h + 1) * LANES)
        dq_ref[:, cols] = (_rope_cols(dq[:, cols], dc, dsp, dsm, DIFF_ROT // 2) * diff_scale).astype(BF16)

    q = jnp.dot(qn, wqb_ref[...], preferred_element_type=F32)
    mla_scale = MLA_QK ** -0.5 * LOG2_E
    for h in range(MLA_HEADS):
        base = h * MLA_QK
        qm_ref[h, :, 0:MLA_NOPE] = (q[:, base:base + MLA_NOPE] * mla_scale).astype(BF16)
        qr = _rope_tail(q[:, base + MLA_QK - LANES:base + MLA_QK], mc, msp, msm)
        qm_ref[h, :, MLA_NOPE:MLA_QK] = (qr * mla_scale).astype(BF16)

    dk = proj(2048, 2560)
    for h in range(DIFF_HEADS):
        cols = slice(h * LANES, (h + 1) * LANES)
        dk_ref[:, cols] = _rope_cols(dk[:, cols], dc, dsp, dsm, DIFF_ROT // 2).astype(BF16)

    ones = jnp.ones((BF16_ROWS, x_ref.shape[0]), BF16)
    kv = jnp.dot(kvn, wkvb_ref[...], preferred_element_type=F32)
    for h in range(MLA_HEADS):
        base = h * 2 * LANES
        km_ref[h, :, 0:MLA_NOPE] = kv[:, base:base + LANES].astype(BF16)
        km_ref[h, :, MLA_NOPE:MLA_QK] = kr
        vmt_ref[h, 0:V_DIM, :] = kv[:, base + LANES:base + 2 * LANES].T.astype(BF16)
        vmt_ref[h, V_DIM:V_ROWS, :] = ones

    dv = proj(2560, 3072)
    for h in range(DIFF_HEADS):
        dvt_ref[h, 0:V_DIM, :] = dv[:, h * LANES:(h + 1) * LANES].T.astype(BF16)
        dvt_ref[h, V_DIM:V_ROWS, :] = ones

    lrux_ref[...] = proj(0, 768)
    lruy_ref[...] = proj(768, 1536)


def _inproj(x, g, win, wqb, wkvb, gqa, gkva, dtab, mtab, layer):
    s = x.shape[0]
    tm = TM_PROJ
    rows = lambda w: pl.BlockSpec((tm, w), lambda i: (i, 0))
    heads = pl.BlockSpec((MLA_HEADS, tm, MLA_QK), lambda i: (0, i, 0))
    vt = lambda nh: pl.BlockSpec((nh, V_ROWS, tm), lambda i: (0, 0, i))
    tab = rows(LANES)
    return pl.pallas_call(
        _inproj_kernel,
        grid=(s // tm,),
        in_specs=[rows(D_MODEL), _resident((1, D_MODEL)), _layer_resident(win, layer),
                  _layer_resident(wqb, layer), _layer_resident(wkvb, layer),
                  _resident((1, MLA_Q_RANK)), _resident((1, MLA_KV_RANK)),
                  tab, tab, tab, tab, tab, tab],
        out_specs=[rows(LRU_WIDTH), rows(LRU_WIDTH), rows(DIFF_QK), rows(DIFF_QK), vt(DIFF_HEADS),
                   heads, heads, vt(MLA_HEADS)],
        out_shape=[jax.ShapeDtypeStruct((s, LRU_WIDTH), F32), jax.ShapeDtypeStruct((s, LRU_WIDTH), F32),
                   jax.ShapeDtypeStruct((s, DIFF_QK), BF16), jax.ShapeDtypeStruct((s, DIFF_QK), BF16),
                   jax.ShapeDtypeStruct((DIFF_HEADS, V_ROWS, s), BF16),
                   jax.ShapeDtypeStruct((MLA_HEADS, s, MLA_QK), BF16),
                   jax.ShapeDtypeStruct((MLA_HEADS, s, MLA_QK), BF16),
                   jax.ShapeDtypeStruct((MLA_HEADS, V_ROWS, s), BF16)],
        compiler_params=_params("parallel"),
        name="inproj",
    )(x, g, win, wqb, wkvb, gqa, gkva, *dtab, *mtab)


def _lru_kernel(x_ref, y_ref, cw_ref, cb_ref, wr_ref, br_ref, wi_ref, bi_ref, lam_ref, o_ref,
                xext_sc, a_sc, b_sc, carry_sc):
    ts = x_ref.shape[0]
    pad = SUBLANES

    @pl.when(pl.program_id(0) == 0)
    def _():
        xext_sc[0:pad, :] = jnp.zeros((pad, LRU_WIDTH), F32)
        carry_sc[...] = jnp.zeros_like(carry_sc)

    xext_sc[pad:pad + ts, :] = x_ref[...]
    xc = cb_ref[...] + cw_ref[CONV_W - 1:CONV_W, :] * xext_sc[pad:pad + ts, :]
    for k in range(1, CONV_W):
        xc = xc + cw_ref[CONV_W - 1 - k:CONV_W - k, :] * xext_sc[pad - k:pad - k + ts, :]
    xext_sc[0:pad, :] = x_ref[ts - pad:ts, :]

    xcb = xc.astype(BF16)
    r_parts, i_parts = [], []
    for h in range(LRU_BLOCKS):
        cols = slice(h * LRU_BLOCK_W, (h + 1) * LRU_BLOCK_W)
        r_parts.append(jnp.dot(xcb[:, cols], wr_ref[h], preferred_element_type=F32))
        i_parts.append(jnp.dot(xcb[:, cols], wi_ref[h], preferred_element_type=F32))
    r = jax.nn.sigmoid(jnp.concatenate(r_parts, axis=1) + br_ref[...])
    ig = jax.nn.sigmoid(jnp.concatenate(i_parts, axis=1) + bi_ref[...])

    z = -lam_ref[...]
    softplus = jnp.maximum(z, 0.0) + jnp.log1p(jnp.exp(-jnp.abs(z)))
    log_a = (-LRU_C) * r * softplus
    a = jnp.exp(log_a)
    t = 1.0 - a * a
    b = jnp.where(t > 0.0, t * lax.rsqrt(t), 0.0) * (ig * xc)

    row = lax.broadcasted_iota(jnp.int32, (ts, 1), 0) & (SUBLANES - 1)
    d = 1
    while d < SUBLANES:
        keep = row >= d
        b = jnp.where(keep, a * pltpu.roll(b, d, 0) + b, b)
        a = jnp.where(keep, a * pltpu.roll(a, d, 0), a)
        d *= 2
    a_sc[...] = a
    b_sc[...] = b

    carry = carry_sc[...]
    for gidx in range(ts // SUBLANES):
        rows = slice(gidx * SUBLANES, (gidx + 1) * SUBLANES)
        h8 = b_sc[rows, :] + a_sc[rows, :] * carry
        b_sc[rows, :] = h8
        carry = h8[SUBLANES - 1:SUBLANES, :]
    carry_sc[...] = carry
    o_ref[...] = (b_sc[...] * jax.nn.gelu(y_ref[...], approximate=True)).astype(BF16)


def _lru(lrux, lruy, cw, cb, wr, br, wi, bi, lam, layer):
    s = lrux.shape[0]
    ts = TS_LRU
    rows = pl.BlockSpec((ts, LRU_WIDTH), lambda i: (i, 0))
    vec = _resident((1, LRU_WIDTH))
    return pl.pallas_call(
        _lru_kernel,
        grid=(s // ts,),
        in_specs=[rows, rows, _resident((CONV_W, LRU_WIDTH)), vec, _layer_resident(wr, layer), vec,
                  _layer_resident(wi, layer), vec, vec],
        out_specs=rows,
        out_shape=jax.ShapeDtypeStruct((s, LRU_WIDTH), BF16),
        scratch_shapes=[pltpu.VMEM((ts + SUBLANES, LRU_WIDTH), F32),
                        pltpu.VMEM((ts, LRU_WIDTH), F32),
                        pltpu.VMEM((ts, LRU_WIDTH), F32),
                        pltpu.VMEM((1, LRU_WIDTH), F32)],
        compiler_params=_params("arbitrary"),
        name="rglru",
    )(lrux, lruy, cw, cb, wr, br, wi, bi, lam)


NEG = -1e30
LOG2_E = math.log2(math.e)


def _pair_tables(n_blk):
    qi, kj = [], []
    for i in range(n_blk):
        for j in range(i + 1):
            qi.append(i)
            kj.append(j)
    return jnp.asarray(qi, jnp.int32), jnp.asarray(kj, jnp.int32)


def _online_softmax_step(st, vt1, m_sc, acc_sc, idx, qs):
    m_prev = m_sc[idx, :, qs]
    m_new = jnp.maximum(m_prev, jnp.max(st, axis=0, keepdims=True))
    alpha = jnp.exp2(m_prev - m_new)
    pt = jnp.exp2((st - m_new).astype(BF16))
    acc_sc[idx, :, qs] = alpha * acc_sc[idx, :, qs] + jnp.dot(vt1, pt, preferred_element_type=F32)
    m_sc[idx, :, qs] = m_new


def _block_parts(diagonal):
    full, half = slice(0, T_ATT), T_ATT // 2
    if not diagonal:
        return [(full, full)]
    return [(slice(0, half), full), (slice(half, T_ATT), slice(half, T_ATT))]


def _kq(k, q):
    return lax.dot_general(k, q, (((1,), (1,)), ((), ())), preferred_element_type=F32)


def _chunk_mask(t):
    shift = CHUNK.bit_length() - 1
    key = lax.broadcasted_iota(jnp.int32, (t, t), 0) >> shift
    qry = lax.broadcasted_iota(jnp.int32, (t, t), 1) >> shift
    return key <= qry


def _pipelined_maps(scores, values, m_sc, acc_sc, n_maps, diagonal):
    steps = [(idx, ks, qs) for ks, qs in _block_parts(diagonal) for idx in range(n_maps)]
    pending = [scores(*step) for step in steps[:QK_AHEAD]]
    for n, (idx, ks, qs) in enumerate(steps):
        if n + QK_AHEAD < len(steps):
            pending.append(scores(*steps[n + QK_AHEAD]))
        _online_softmax_step(pending.pop(0), values(idx, ks), m_sc, acc_sc, idx, qs)


def _init_stats(m_sc, acc_sc):
    m_sc[...] = jnp.full(m_sc.shape, NEG, F32)
    acc_sc[...] = jnp.zeros(acc_sc.shape, F32)


def _normalized(acc_sc, idx):
    return acc_sc[idx, 0:V_DIM, :] / acc_sc[idx, V_DIM:V_DIM + 1, :]


def _diff_attn_kernel(qi_ref, kj_ref, q_ref, k_ref, vt_ref, lq1_ref, lk1_ref, lq2_ref, lk2_ref,
                      gsub_ref, linit_ref, o_ref, m_sc, acc_sc):
    t = pl.program_id(0)
    i = qi_ref[t]
    j = kj_ref[t]

    @pl.when(j == 0)
    def _():
        _init_stats(m_sc, acc_sc)

    lane = lax.broadcasted_iota(jnp.int32, (1, LANES), 1)
    first_map = lane < DIFF_HEAD_DIM

    def block(diagonal):
        mask = _chunk_mask(T_ATT) if diagonal else None

        def scores(idx, ks, qs):
            h, m = divmod(idx, 2)
            cols = slice(h * LANES, (h + 1) * LANES)
            q = q_ref[qs, cols]
            zero = jnp.zeros_like(q)
            qm = jnp.where(first_map, q, zero) if m == 0 else jnp.where(first_map, zero, q)
            st = _kq(k_ref[ks, cols], qm)
            return jnp.where(mask[ks, qs], st, NEG) if diagonal else st

        _pipelined_maps(scores, lambda idx, ks: vt_ref[idx // 2, :, ks], m_sc, acc_sc,
                        2 * DIFF_HEADS, diagonal)

    @pl.when(j < i)
    def _():
        block(False)

    @pl.when(j == i)
    def _():
        block(True)
        linit = linit_ref[:, 0:1]
        lam = (jnp.exp(jnp.sum(lq1_ref[...] * lk1_ref[...], axis=1, keepdims=True))
               - jnp.exp(jnp.sum(lq2_ref[...] * lk2_ref[...], axis=1, keepdims=True)) + linit)
        for h in range(DIFF_HEADS):
            ot = _normalized(acc_sc, 2 * h) - lam * _normalized(acc_sc, 2 * h + 1)
            o = _rms(ot.T, gsub_ref[...], SUBLN_EPS) * (1.0 - linit)
            o_ref[:, h * LANES:(h + 1) * LANES] = o.astype(BF16)


def _diff_attention(dq, dk, dvt, lq1, lk1, lq2, lk2, gsub, linit, qi, kj):
    s = dq.shape[0]
    t = T_ATT
    qspec = pl.BlockSpec((t, DIFF_QK), lambda p, qi, kj: (qi[p], 0))
    kspec = pl.BlockSpec((t, DIFF_QK), lambda p, qi, kj: (kj[p], 0))
    vspec = pl.BlockSpec((DIFF_HEADS, V_ROWS, t), lambda p, qi, kj: (0, 0, kj[p]))
    small = lambda w: pl.BlockSpec((1, w), lambda p, qi, kj: (0, 0))
    nm = 2 * DIFF_HEADS
    return pl.pallas_call(
        _diff_attn_kernel,
        grid_spec=pltpu.PrefetchScalarGridSpec(
            num_scalar_prefetch=2,
            grid=(qi.shape[0],),
            in_specs=[qspec, kspec, vspec, small(DIFF_HEAD_DIM), small(DIFF_HEAD_DIM),
                      small(DIFF_HEAD_DIM), small(DIFF_HEAD_DIM), small(DIFF_V_DIM), small(LANES)],
            out_specs=qspec,
            scratch_shapes=[pltpu.VMEM((nm, 1, t), F32), pltpu.VMEM((nm, V_ROWS, t), F32)]),
        out_shape=jax.ShapeDtypeStruct((s, DIFF_OUT), BF16),
        compiler_params=_params("arbitrary"),
        name="diff_attention",
    )(qi, kj, dq, dk, dvt, lq1, lk1, lq2, lk2, gsub, linit)


def _mla_attn_kernel(qi_ref, kj_ref, q_ref, k_ref, vt_ref, o_ref, m_sc, acc_sc):
    t = pl.program_id(0)
    i = qi_ref[t]
    j = kj_ref[t]

    @pl.when(j == 0)
    def _():
        _init_stats(m_sc, acc_sc)

    def block(diagonal):
        mask = _chunk_mask(T_ATT) if diagonal else None

        def scores(h, ks, qs):
            st = _kq(k_ref[h, ks, :], q_ref[h, qs, :])
            return jnp.where(mask[ks, qs], st, NEG) if diagonal else st

        _pipelined_maps(scores, lambda h, ks: vt_ref[h, :, ks], m_sc, acc_sc, MLA_HEADS, diagonal)

    @pl.when(j < i)
    def _():
        block(False)

    @pl.when(j == i)
    def _():
        block(True)
        for h in range(MLA_HEADS):
            o_ref[:, h * LANES:(h + 1) * LANES] = _normalized(acc_sc, h).T.astype(BF16)


def _mla_attention(qm, km, vmt, qi, kj):
    s = qm.shape[1]
    t = T_ATT
    qspec = pl.BlockSpec((MLA_HEADS, t, MLA_QK), lambda p, qi, kj: (0, qi[p], 0))
    kspec = pl.BlockSpec((MLA_HEADS, t, MLA_QK), lambda p, qi, kj: (0, kj[p], 0))
    vspec = pl.BlockSpec((MLA_HEADS, V_ROWS, t), lambda p, qi, kj: (0, 0, kj[p]))
    ospec = pl.BlockSpec((t, MLA_OUT), lambda p, qi, kj: (qi[p], 0))
    return pl.pallas_call(
        _mla_attn_kernel,
        grid_spec=pltpu.PrefetchScalarGridSpec(
            num_scalar_prefetch=2,
            grid=(qi.shape[0],),
            in_specs=[qspec, kspec, vspec],
            out_specs=ospec,
            scratch_shapes=[pltpu.VMEM((MLA_HEADS, 1, t), F32), pltpu.VMEM((MLA_HEADS, V_ROWS, t), F32)]),
        out_shape=jax.ShapeDtypeStruct((s, MLA_OUT), BF16),
        compiler_params=_params("arbitrary"),
        name="mla_attention",
    )(qi, kj, qm, km, vmt)


def _outproj_kernel(x_ref, a_ref, b_ref, c_ref, w_ref, o_ref, wb_sc):
    @pl.when(pl.program_id(0) == 0)
    def _():
        wb_sc[...] = w_ref[...].astype(BF16)

    acc = jnp.dot(a_ref[...], wb_sc[0:LRU_WIDTH, :], preferred_element_type=F32)
    acc = acc + jnp.dot(b_ref[...], wb_sc[LRU_WIDTH:LRU_WIDTH + DIFF_OUT, :], preferred_element_type=F32)
    acc = acc + jnp.dot(c_ref[...], wb_sc[LRU_WIDTH + DIFF_OUT:D_MODEL, :], preferred_element_type=F32)
    o_ref[...] = x_ref[...] + acc


def _outproj(x, a, b, c, w, layer):
    s = x.shape[0]
    tm = TM_OUT
    rows = lambda wd: pl.BlockSpec((tm, wd), lambda i: (i, 0))
    return pl.pallas_call(
        _outproj_kernel,
        grid=(s // tm,),
        in_specs=[rows(D_MODEL), rows(LRU_WIDTH), rows(DIFF_OUT), rows(MLA_OUT),
                  _layer_resident(w, layer)],
        out_specs=rows(D_MODEL),
        out_shape=jax.ShapeDtypeStruct((s, D_MODEL), F32),
        scratch_shapes=[pltpu.VMEM((D_MODEL, D_MODEL), BF16)],
        compiler_params=_params("arbitrary"),
        name="outproj",
    )(x, a, b, c, w)


def _ffn_kernel(x_hbm, g_ref, wg_ref, wu_ref, wd_ref, gf_ref, o_ref, x_sc, hb_sc, x_sem, *, final):
    i = pl.program_id(0)
    f = pl.program_id(1)
    tm = x_sc.shape[0]

    def x_copy(tile):
        return pltpu.make_async_copy(x_hbm.at[pl.ds(tile * tm, tm), :], x_sc, x_sem)

    @pl.when((i == 0) & (f == 0))
    def _():
        x_copy(0).start()

    @pl.when(f == 0)
    def _():
        x_copy(i).wait()
        x = x_sc[...]
        hb_sc[...] = _rms(x, g_ref[...], NORM_EPS).astype(BF16)
        o_ref[...] = x

    @pl.when((f == 1) & (i + 1 < pl.num_programs(0)))
    def _():
        x_copy(i + 1).start()

    hb = hb_sc[...]
    gate = jnp.dot(hb, wg_ref[...].astype(BF16), preferred_element_type=F32)
    up = jnp.dot(hb, wu_ref[...].astype(BF16), preferred_element_type=F32)
    act = (gate * jax.nn.sigmoid(gate) * up).astype(BF16)
    o_ref[...] += jnp.dot(act, wd_ref[...].astype(BF16), preferred_element_type=F32)

    if final:
        @pl.when(f == pl.num_programs(1) - 1)
        def _():
            o_ref[...] = _rms(o_ref[...], gf_ref[...], NORM_EPS)


def _ffn(x, g, wg, wu, wd, gf, layer, final):
    s = x.shape[0]
    tm, tf = TM_FFN, TF_FFN
    assert D_FF // tf >= 2
    rows = pl.BlockSpec((tm, D_MODEL), lambda i, f: (i, 0))
    vec = pl.BlockSpec((1, D_MODEL), lambda i, f: (0, 0))
    return pl.pallas_call(
        functools.partial(_ffn_kernel, final=final),
        grid=(s // tm, D_FF // tf),
        in_specs=[pl.BlockSpec(memory_space=pl.ANY), vec,
                  pl.BlockSpec((None, D_MODEL, tf), lambda i, f: (layer, 0, f)),
                  pl.BlockSpec((None, D_MODEL, tf), lambda i, f: (layer, 0, f)),
                  pl.BlockSpec((None, tf, D_MODEL), lambda i, f: (layer, f, 0)), vec],
        out_specs=rows,
        out_shape=jax.ShapeDtypeStruct((s, D_MODEL), F32),
        scratch_shapes=[pltpu.VMEM((tm, D_MODEL), F32), pltpu.VMEM((tm, D_MODEL), BF16),
                        pltpu.SemaphoreType.DMA(())],
        compiler_params=_params("arbitrary", "arbitrary"),
        name="ffn",
    )(x, g, wg, wu, wd, gf)


def kernel(x, positions, g_mix, w_in, conv_w, conv_b, w_r, b_r, w_i, b_i, lru_lambda, lam_q1, lam_k1,
           lam_q2, lam_k2, g_sub, g_q_a, w_q_b, g_kv_a, w_kv_b, w_out, g_ffn, w_gate, w_up, w_down,
           g_final):
    batch, seq, _ = x.shape
    assert batch == 1 and seq % TM_FFN == 0 and seq % T_ATT == 0
    xs = x.reshape(seq, D_MODEL)
    pos_col = positions.reshape(seq, 1)

    dtab = _rope_tables(pos_col, DIFF_HEAD_DIM, DIFF_ROT)
    mtab = _rope_tables(pos_col, LANES, MLA_ROPE)
    qi, kj = _pair_tables(seq // T_ATT)

    win = w_in.astype(BF16)
    wqb = w_q_b.astype(BF16)
    wkvb = w_kv_b.astype(BF16)
    wr = w_r.astype(BF16)
    wi = w_i.astype(BF16)
    row = lambda v: v.reshape(1, -1)

    for l in range(DEPTH):
        lambda_init = 0.8 - 0.6 * math.exp(-0.3 * l)
        linit = jnp.full((1, LANES), lambda_init, F32)
        lrux, lruy, dq, dk, dvt, qm, km, vmt = _inproj(
            xs, row(g_mix[l]), win, wqb, wkvb, row(g_q_a[l]), row(g_kv_a[l]), dtab, mtab, l)
        out_a = _lru(lrux, lruy, conv_w[l], row(conv_b[l]), wr, row(b_r[l]), wi, row(b_i[l]),
                     row(lru_lambda[l]), l)
        out_b = _diff_attention(dq, dk, dvt, row(lam_q1[l]), row(lam_k1[l]), row(lam_q2[l]),
                                row(lam_k2[l]), row(g_sub[l]), linit, qi, kj)
        out_c = _mla_attention(qm, km, vmt, qi, kj)
        xs = _outproj(xs, out_a, out_b, out_c, w_out, l)
        xs = _ffn(xs, row(g_ffn[l]), w_gate, w_up, w_down, row(g_final), l, final=(l == DEPTH - 1))
    return xs.reshape(batch, seq, D_MODEL)
```
